```python
import jax
import jax.numpy as jnp
from jax import lax
import numpy as np

D_MODEL = 1024
BATCH = 4
SEQ = 8192
DEPTH = 1
DEC_BATCH = 128
DEC_SEQ = 8
PAST_LEN = 8192
PAGE_SIZE = 128

HEAD_DIM = 64
MIX_WIDTH = D_MODEL
RW_WIDTH = MIX_WIDTH // 2
AT_WIDTH = MIX_WIDTH - RW_WIDTH
RW_HEADS = RW_WIDTH // HEAD_DIM
AT_HEADS = AT_WIDTH // HEAD_DIM
DECAY_LORA = 32
AAA_LORA = 32
GATE_LORA = 64
RW_COLS = 3 * RW_WIDTH + DECAY_LORA + AAA_LORA + GATE_LORA
IN_COLS = RW_COLS + 3 * AT_WIDTH
GN_EPS = 64e-5
RMS_EPS = 1e-6
WINDOWS = (128, 512, 2048)
DILATIONS = (1, 4, 16)
MAX_WINDOW = max(WINDOWS)
ATT_BLOCK = 128
N_GROUPS = 4
EXPERTS_PER_GROUP = 8
N_EXPERTS = N_GROUPS * EXPERTS_PER_GROUP
TOP_K = 2
D_EXPERT = 256
PLE_DIM = 256

kernel_name = 'hymba_rwkv7_dilated_alibi_hmoe_step'


def rmsnorm(x, g):
    xf = x.astype(jnp.float32)
    y = xf * lax.rsqrt(jnp.mean(xf * xf, axis=-1, keepdims=True) + RMS_EPS)
    return (y * g.astype(jnp.float32)).astype(x.dtype)


def alibi_slopes():
    return jnp.exp2(-8.0 * jnp.arange(1, AT_HEADS + 1, dtype=jnp.float32) / AT_HEADS)


def wkv_scan(s0, r, decay, k, v, kk, a):
    def step(s, inp):
        r_t, w_t, k_t, v_t, kk_t, a_t = inp
        s_kk = jnp.einsum('bhvk,bhk->bhv', s, kk_t)
        s = (s * w_t[:, :, None, :]
             - s_kk[..., None] * (kk_t * a_t)[:, :, None, :]
             + v_t[..., None] * k_t[:, :, None, :])
        return s, jnp.einsum('bhvk,bhk->bhv', s, r_t)
    xs = tuple(jnp.swapaxes(z.astype(jnp.float32), 0, 1) for z in (r, decay, k, v, kk, a))
    s_final, ys = lax.scan(step, s0.astype(jnp.float32), xs)
    return jnp.swapaxes(ys, 0, 1), s_final


def rwkv7_group(pr, pr_prev, wkv0, prm):
    B, T, _ = pr.shape
    xm = pr + (pr_prev - pr) * prm['mu']
    c = [int(e) for e in np.cumsum([0, RW_WIDTH, RW_WIDTH, RW_WIDTH, DECAY_LORA, AAA_LORA, GATE_LORA])]
    xr, xk, xv, xw, xa, xg = [xm[..., c[i]:c[i + 1]] for i in range(6)]
    heads = lambda z: z.reshape(B, T, RW_HEADS, HEAD_DIM)
    w = -jax.nn.softplus(-(prm['w0'] + jnp.tanh(xw) @ prm['w2'])) - 0.5
    decay = jnp.exp(-jnp.exp(w.astype(jnp.float32)))
    a = jax.nn.sigmoid(prm['a0'] + xa @ prm['a2'])
    g = jax.nn.sigmoid(xg) @ prm['g2']
    kk = heads(xk * prm['k_k']).astype(jnp.float32)
    kk = kk / jnp.maximum(jnp.sqrt(jnp.sum(kk * kk, axis=-1, keepdims=True)), 1e-12)
    k = heads(xk * (1.0 + (a - 1.0) * prm['k_a']))
    r, v, a = heads(xr), heads(xv), heads(a)
    y, wkv_new = wkv_scan(wkv0, r, heads(decay), k, v, kk, a)
    mean = jnp.mean(y, axis=-1, keepdims=True)
    var = jnp.mean(jnp.square(y - mean), axis=-1, keepdims=True)
    yn = ((y - mean) * lax.rsqrt(var + GN_EPS)).reshape(B, T, RW_WIDTH) * prm['ln_x_w'] + prm['ln_x_b']
    bonus = (jnp.sum(r * k * prm['r_k'], axis=-1, keepdims=True) * v).reshape(B, T, RW_WIDTH)
    return (yn + bonus) * g, wkv_new


def dilated_branch_prompt(q, k, v, dil, n_steps, slopes):
    B, S, H, Dh = q.shape
    span = ATT_BLOCK * dil
    s_pad = -(-S // span) * span
    n_blk = s_pad // span

    def to_blocks(z):
        z = jnp.pad(z.astype(jnp.float32), ((0, 0), (0, s_pad - S), (0, 0), (0, 0)))
        z = jnp.swapaxes(z.reshape(B, s_pad // dil, dil, H, Dh), 1, 2)
        return z.reshape(B, dil, n_blk, ATT_BLOCK, H, Dh)

    def with_prev(z):
        prev = jnp.pad(z, ((0, 0), (0, 0), (1, 0), (0, 0), (0, 0), (0, 0)))[:, :, :-1]
        return jnp.concatenate([prev, z], axis=3)

    def from_blocks(z):
        rest = z.shape[4:]
        z = z.reshape((B, dil, s_pad // dil) + rest)
        return jnp.swapaxes(z, 1, 2).reshape((B, s_pad) + rest)[:, :S]

    qb = to_blocks(q)
    kb, vb = with_prev(to_blocks(k)), with_prev(to_blocks(v))
    scores = jnp.einsum('brnqhd,brnkhd->brnhqk', qb, kb) * (HEAD_DIM ** -0.5)
    qi = jnp.arange(ATT_BLOCK)[:, None]
    ki = jnp.arange(2 * ATT_BLOCK)[None, :]
    steps = ATT_BLOCK + qi - ki
    blk = jnp.arange(n_blk)[:, None, None]
    valid = (steps >= 0) & (steps <= n_steps) & ((blk > 0) | (ki >= ATT_BLOCK))
    bias = -slopes[:, None, None] * (steps * dil).astype(jnp.float32)
    scores = jnp.where(valid[:, None], scores + bias, -jnp.inf)
    m = jnp.max(scores, axis=-1, keepdims=True)
    e = jnp.exp(scores - m)
    den = jnp.sum(e, axis=-1)
    o = jnp.einsum('brnhqk,brnkhd->brnqhd', e, vb) / jnp.swapaxes(den, -1, -2)[..., None]
    lse = jnp.swapaxes(m[..., 0] + jnp.log(den), -1, -2)
    return from_blocks(o), from_blocks(lse)


def dilated_branch_sample(q, k_all, v_all, dil, n_steps, slopes):
    B, T, H, Dh = q.shape
    n_prev = k_all.shape[1] - T
    steps = jnp.arange(n_steps + 1)
    idx = n_prev + jnp.arange(T)[:, None] - steps[None, :] * dil
    valid = idx >= 0
    idx = jnp.maximum(idx, 0)
    kg = k_all[:, idx].astype(jnp.float32)
    vg = v_all[:, idx].astype(jnp.float32)
    scores = (jnp.einsum('bthd,btkhd->bthk', q.astype(jnp.float32), kg) * (HEAD_DIM ** -0.5)
              - slopes[:, None] * (steps * dil).astype(jnp.float32))
    scores = jnp.where(valid[None, :, None, :], scores, -jnp.inf)
    m = jnp.max(scores, axis=-1, keepdims=True)
    e = jnp.exp(scores - m)
    den = jnp.sum(e, axis=-1)
    o = jnp.einsum('bthk,btkhd->bthd', e, vg) / den[..., None]
    return o, m[..., 0] + jnp.log(den)


def mix_branches(outs, lses):
    wts = jax.nn.softmax(jnp.stack(lses), axis=0)
    return jnp.sum(wts[..., None] * jnp.stack(outs), axis=0)


def hier_moe(h, prm):
    B, T, D = h.shape
    hf = h.reshape(B * T, D)
    n = hf.shape[0]
    glog = (hf @ prm['w_group'] + prm['b_group']).astype(jnp.float32)
    gsel = jnp.argmax(glog, axis=-1)
    gw = jnp.take_along_axis(jax.nn.softmax(glog, axis=-1), gsel[:, None], axis=-1)
    elog = (hf @ prm['w_expert_router'] + prm['b_expert_router']).astype(jnp.float32)
    elog = elog.reshape(n, N_GROUPS, EXPERTS_PER_GROUP)[jnp.arange(n), gsel]
    tv, ti = lax.top_k(elog, TOP_K)
    tw = jax.nn.softmax(tv, axis=-1) * gw
    eid = gsel[:, None] * EXPERTS_PER_GROUP + ti
    comb = jnp.einsum('nk,nke->ne', tw, jax.nn.one_hot(eid, N_EXPERTS, dtype=jnp.float32))

    def expert_step(acc, ws):
        wg, wu, wd, c = ws
        out = (jax.nn.silu(hf @ wg) * (hf @ wu)) @ wd
        return acc + c[:, None] * out.astype(jnp.float32), None

    acc, _ = lax.scan(expert_step, jnp.zeros((n, D), jnp.float32),
                      (prm['w_gate'], prm['w_up'], prm['w_down'], comb.T))
    return acc.reshape(B, T, D).astype(h.dtype)


def decoder_layer(x, p, shift0, wkv0, attend, prm):
    B, T, _ = x.shape
    h = rmsnorm(x, prm['norm_mix'])
    proj = h @ prm['w_in']
    pr, qkv = proj[..., :RW_COLS], proj[..., RW_COLS:]
    pr_prev = jnp.concatenate([shift0[:, None].astype(pr.dtype), pr[:, :-1]], axis=1)
    rw_out, wkv_new = rwkv7_group(pr, pr_prev, wkv0, prm)
    q, k, v = [qkv[..., i * AT_WIDTH:(i + 1) * AT_WIDTH].reshape(B, T, AT_HEADS, HEAD_DIM) for i in range(3)]
    at_out = rmsnorm(attend(q, k, v).reshape(B, T, AT_WIDTH), prm['attn_gain'])
    mixed = jnp.concatenate([rw_out.astype(x.dtype), at_out.astype(x.dtype)], axis=-1)
    x = x + mixed @ prm['w_out']
    x = x + hier_moe(rmsnorm(x, prm['norm_ffn']), prm)
    gate = jax.nn.sigmoid(rmsnorm(x, prm['norm_ple']) @ prm['w_ple_gate'])
    x = x + (p @ prm['w_ple']) * gate
    return x, pr[:, -1], wkv_new, k, v


def setup_inputs(seed: int = 0) -> dict:
    key = jax.random.key(seed)
    keys = iter(jax.random.split(key, 64))
    f32 = jnp.float32
    L = DEPTH
    win_buf = min(MAX_WINDOW, PAST_LEN)

    def nrm(shape, scale):
        return scale * jax.random.normal(next(keys), shape, f32)

    def gain(shape):
        return 1.0 + nrm(shape, 0.05)

    return {
        'x_prompt': nrm((BATCH, SEQ, D_MODEL), 1.0),
        'x_sample': nrm((DEC_BATCH, DEC_SEQ, D_MODEL), 1.0),
        'state_wkv': nrm((L, DEC_BATCH, RW_HEADS, HEAD_DIM, HEAD_DIM), 0.3),
        'state_shift': nrm((L, DEC_BATCH, RW_COLS), 1.0),
        'cache_k_win': nrm((L, DEC_BATCH, win_buf, AT_HEADS, HEAD_DIM), 1.0),
        'cache_v_win': nrm((L, DEC_BATCH, win_buf, AT_HEADS, HEAD_DIM), 1.0),
        'p_prompt': nrm((L, BATCH, SEQ, PLE_DIM), 1.0),
        'p_sample': nrm((L, DEC_BATCH, DEC_SEQ, PLE_DIM), 1.0),
        'norm_mix': gain((L, D_MODEL)),
        'w_in': nrm((L, D_MODEL, IN_COLS), D_MODEL ** -0.5),
        'mu': jax.random.uniform(next(keys), (L, RW_COLS), f32),
        'w0': jax.random.uniform(next(keys), (L, RW_WIDTH), f32, -6.0, 1.0),
        'w2': nrm((L, DECAY_LORA, RW_WIDTH), 0.1),
        'a0': nrm((L, RW_WIDTH), 0.5),
        'a2': nrm((L, AAA_LORA, RW_WIDTH), 0.1),
        'g2': nrm((L, GATE_LORA, RW_WIDTH), GATE_LORA ** -0.5),
        'k_k': 0.85 + nrm((L, RW_WIDTH), 0.05),
        'k_a': gain((L, RW_WIDTH)),
        'r_k': nrm((L, RW_HEADS, HEAD_DIM), 0.1),
        'ln_x_w': gain((L, RW_WIDTH)),
        'ln_x_b': nrm((L, RW_WIDTH), 0.02),
        'attn_gain': gain((L, AT_WIDTH)),
        'w_out': nrm((L, MIX_WIDTH, D_MODEL), MIX_WIDTH ** -0.5),
        'norm_ffn': gain((L, D_MODEL)),
        'w_group': nrm((L, D_MODEL, N_GROUPS), D_MODEL ** -0.5),
        'b_group': nrm((L, N_GROUPS), 0.01),
        'w_expert_router': nrm((L, D_MODEL, N_EXPERTS), D_MODEL ** -0.5),
        'b_expert_router': nrm((L, N_EXPERTS), 0.01),
        'w_gate': nrm((L, N_EXPERTS, D_MODEL, D_EXPERT), D_MODEL ** -0.5),
        'w_up': nrm((L, N_EXPERTS, D_MODEL, D_EXPERT), D_MODEL ** -0.5),
        'w_down': nrm((L, N_EXPERTS, D_EXPERT, D_MODEL), D_EXPERT ** -0.5),
        'norm_ple': gain((L, D_MODEL)),
        'w_ple': nrm((L, PLE_DIM, D_MODEL), PLE_DIM ** -0.5),
        'w_ple_gate': nrm((L, D_MODEL, D_MODEL), D_MODEL ** -0.5),
        'norm_final': gain((D_MODEL,)),
    }


def reference(x_prompt, x_sample, state_wkv, state_shift, cache_k_win, cache_v_win, p_prompt, p_sample,
              norm_mix, w_in, mu, w0, w2, a0, a2, g2, k_k, k_a, r_k, ln_x_w, ln_x_b, attn_gain, w_out,
              norm_ffn, w_group, b_group, w_expert_router, b_expert_router, w_gate, w_up, w_down,
              norm_ple, w_ple, w_ple_gate, norm_final):
    slopes = alibi_slopes()
    branches = tuple((d, w // d) for w, d in zip(WINDOWS, DILATIONS))
    n_prompt = x_prompt.shape[0]

    def attend_prompt(q, k, v):
        res = [dilated_branch_prompt(q, k, v, d, n, slopes) for d, n in branches]
        return mix_branches([r[0] for r in res], [r[1] for r in res])

    yp, ys = x_prompt, x_sample
    wkv_p, shift_p, kwin_p, vwin_p = [], [], [], []
    wkv_s, shift_s, kwin_s, vwin_s = [], [], [], []
    for i in range(DEPTH):
        prm = {
            'norm_mix': norm_mix[i], 'w_in': w_in[i], 'mu': mu[i], 'w0': w0[i], 'w2': w2[i],
            'a0': a0[i], 'a2': a2[i], 'g2': g2[i], 'k_k': k_k[i], 'k_a': k_a[i], 'r_k': r_k[i],
            'ln_x_w': ln_x_w[i], 'ln_x_b': ln_x_b[i], 'attn_gain': attn_gain[i], 'w_out': w_out[i],
            'norm_ffn': norm_ffn[i], 'w_group': w_group[i], 'b_group': b_group[i],
            'w_expert_router': w_expert_router[i], 'b_expert_router': b_expert_router[i],
            'w_gate': w_gate[i], 'w_up': w_up[i], 'w_down': w_down[i],
            'norm_ple': norm_ple[i], 'w_ple': w_ple[i], 'w_ple_gate': w_ple_gate[i],
        }
        shift0 = jnp.zeros((n_prompt, RW_COLS), x_prompt.dtype)
        wkv0 = jnp.zeros((n_prompt, RW_HEADS, HEAD_DIM, HEAD_DIM), jnp.float32)
        yp, sh, wkv, kp, vp = decoder_layer(yp, p_prompt[i], shift0, wkv0, attend_prompt, prm)
        keep_p = min(MAX_WINDOW, kp.shape[1])
        wkv_p.append(wkv)
        shift_p.append(sh)
        kwin_p.append(kp[:, kp.shape[1] - keep_p:])
        vwin_p.append(vp[:, vp.shape[1] - keep_p:])

        kc, vc = cache_k_win[i], cache_v_win[i]

        def attend_sample(q, k, v, kc=kc, vc=vc):
            k_all = jnp.concatenate([kc.astype(k.dtype), k], axis=1)
            v_all = jnp.concatenate([vc.astype(v.dtype), v], axis=1)
            res = [dilated_branch_sample(q, k_all, v_all, d, n, slopes) for d, n in branches]
            return mix_branches([r[0] for r in res], [r[1] for r in res])

        ys, sh_s, wkv_n, ks, vs = decoder_layer(ys, p_sample[i], state_shift[i], state_wkv[i], attend_sample, prm)
        keep_s = kc.shape[1]
        k_all = jnp.concatenate([kc.astype(ks.dtype), ks], axis=1)
        v_all = jnp.concatenate([vc.astype(vs.dtype), vs], axis=1)
        wkv_s.append(wkv_n)
        shift_s.append(sh_s)
        kwin_s.append(k_all[:, k_all.shape[1] - keep_s:])
        vwin_s.append(v_all[:, v_all.shape[1] - keep_s:])

    y_prompt = rmsnorm(yp, norm_final)
    y_sample = rmsnorm(ys, norm_final)
    return (y_prompt, y_sample,
            jnp.stack(wkv_p), jnp.stack(shift_p), jnp.stack(kwin_p), jnp.stack(vwin_p),
            jnp.stack(wkv_s), jnp.stack(shift_s), jnp.stack(kwin_s), jnp.stack(vwin_s))
```

```python
import functools

import numpy as np
import jax
import jax.numpy as jnp
from jax import lax
from jax.experimental import pallas as pl
from jax.experimental.pallas import tpu as pltpu

F32 = jnp.float32
BF16 = jnp.bfloat16

HEAD_DIM = 64
DECAY_LORA = 32
AAA_LORA = 32
GATE_LORA = 64
GN_EPS = 64e-5
RMS_EPS = 1e-6
WINDOWS = (128, 512, 2048)
DILATIONS = (1, 4, 16)
ATT_BLOCK = 128
N_GROUPS = 4
EXPERTS_PER_GROUP = 8
N_EXPERTS = N_GROUPS * EXPERTS_PER_GROUP
ROUTER_LANES = 128
EXPERT_LANE0 = N_GROUPS
WKV_CHUNK = 64
VMEM_LIMIT = 56 * 1024 * 1024


def _cparams(n_axes):
    return pltpu.CompilerParams(dimension_semantics=("arbitrary",) * n_axes, vmem_limit_bytes=VMEM_LIMIT)


def _full(shape):
    nd = len(shape)
    return pl.BlockSpec(shape, lambda *_: (0,) * nd)


def _bdot(a, b):
    return jnp.dot(a.astype(BF16), b.astype(BF16), preferred_element_type=F32)


def _split2(a):
    hi = a.astype(BF16)
    lo = (a - hi.astype(F32)).astype(BF16)
    return hi, lo


def _hilo_dot(a, b_bf16):
    hi, lo = _split2(a)
    return (jnp.dot(hi, b_bf16, preferred_element_type=F32)
            + jnp.dot(lo, b_bf16, preferred_element_type=F32))


def _rms(x, g):
    return x * lax.rsqrt(jnp.mean(x * x, axis=-1, keepdims=True) + RMS_EPS) * g


def _in_proj_kernel(x_ref, g_ref, w_ref, pr_ref, q_ref, k_ref, v_ref, *, rw_cols, at_width):
    h = _rms(x_ref[...], g_ref[...]).astype(BF16)
    pr_ref[...] = jnp.dot(h, w_ref[:, :rw_cols], preferred_element_type=F32)
    for i, o_ref in enumerate((q_ref, k_ref, v_ref)):
        lo = rw_cols + i * at_width
        o_ref[...] = jnp.dot(h, w_ref[:, lo:lo + at_width], preferred_element_type=F32)


def _in_proj(x, g, w_bf16, rw_cols, at_width, tm):
    n, d = x.shape
    row = lambda w: pl.BlockSpec((tm, w), lambda i: (i, 0))
    return pl.pallas_call(
        functools.partial(_in_proj_kernel, rw_cols=rw_cols, at_width=at_width),
        grid=(n // tm,),
        in_specs=[row(d), _full(g.shape), _full(w_bf16.shape)],
        out_specs=[row(rw_cols), row(at_width), row(at_width), row(at_width)],
        out_shape=[jax.ShapeDtypeStruct((n, rw_cols), F32)] + [jax.ShapeDtypeStruct((n, at_width), F32)] * 3,
        compiler_params=_cparams(1),
        name="in_proj",
    )(x, g, w_bf16)


def _rwkv_prep_kernel(pr_ref, bnd_ref, mu_ref, w0_ref, w2_ref, a0_ref, a2_ref, g2_ref, kk_ref, ka_ref, rk_ref,
                      bd_ref, r_o, lw_o, k_o, v_o, kk_o, kka_o, g_o, bonus_o, *, rw, shift_rows, tiles_per_seq):
    pr = pr_ref[...]
    tm = pr.shape[0]
    if shift_rows == 1:
        first = pl.program_id(0) % tiles_per_seq == 0
        prev_row = jnp.where(first, 0.0, bnd_ref[7:8, :])
        row = lax.broadcasted_iota(jnp.int32, pr.shape, 0)
        prev = jnp.where(row == 0, prev_row, pltpu.roll(pr, 1, 0))
    else:
        prev = jnp.concatenate([bnd_ref[...], pr[:tm - shift_rows]], axis=0)
    xm = pr + (prev - pr) * mu_ref[...]
    xr, xk, xv, xl = xm[:, :rw], xm[:, rw:2 * rw], xm[:, 2 * rw:3 * rw], xm[:, 3 * rw:]
    bd = bd_ref[...]
    z = -(w0_ref[...] + _bdot(jnp.tanh(xl), w2_ref[...]))
    softplus = jnp.maximum(z, 0.0) + jnp.log(1.0 + jnp.exp(-jnp.abs(z)))
    lw = -jnp.exp(-softplus - 0.5)
    a = jax.nn.sigmoid(a0_ref[...] + _bdot(xl, a2_ref[...]))
    g = _bdot(jax.nn.sigmoid(xl), g2_ref[...])
    kk = xk * kk_ref[...]
    kk = kk / jnp.maximum(jnp.sqrt(_hilo_dot(kk * kk, bd)), 1e-12)
    kmod = xk * (1.0 + (a - 1.0) * ka_ref[...])
    bonus = _hilo_dot(xr * kmod * rk_ref[...], bd) * xv
    r_o[...] = xr
    lw_o[...] = lw
    k_o[...] = kmod
    v_o[...] = xv
    kk_o[...] = kk
    kka_o[...] = kk * a
    g_o[...] = g
    bonus_o[...] = bonus


def _rwkv_prep(pr, bnd, prm, rw, tm, shift_rows, tiles_per_seq):
    n, c = pr.shape
    row = lambda w: pl.BlockSpec((tm, w), lambda i: (i, 0))
    if shift_rows == 1:
        bnd_spec = pl.BlockSpec((8, c), lambda i: (jnp.maximum(i * (tm // 8) - 1, 0), 0))
    else:
        bnd_spec = _full(bnd.shape)
    params = [prm[k] for k in ("mu", "w0", "w2p", "a0", "a2p", "g2p", "k_k", "k_a", "r_k", "bd")]
    return pl.pallas_call(
        functools.partial(_rwkv_prep_kernel, rw=rw, shift_rows=shift_rows, tiles_per_seq=tiles_per_seq),
        grid=(n // tm,),
        in_specs=[row(c), bnd_spec] + [_full(p.shape) for p in params],
        out_specs=[row(rw)] * 8,
        out_shape=[jax.ShapeDtypeStruct((n, rw), F32)] * 8,
        compiler_params=_cparams(1),
        name="rwkv_prep",
    )(pr, bnd, *params)


def _cumsum_rows(x):
    n = x.shape[0]
    row = lax.broadcasted_iota(jnp.int32, x.shape, 0)
    s = 1
    while s < n:
        x = x + jnp.where(row >= s, pltpu.roll(x, s, 0), 0.0)
        s *= 2
    return x


def _wkv_chunk(r, lw, k, v, kk, kka, s_heads, mm_dtype):
    c = r.shape[0]
    n_heads = r.shape[1] // HEAD_DIM
    cast = lambda z: z.astype(mm_dtype)
    nt = lambda a_, b_: lax.dot_general(cast(a_), cast(b_), (((1,), (1,)), ((), ())), preferred_element_type=F32)
    nn = lambda a_, b_: jnp.dot(cast(a_), cast(b_), preferred_element_type=F32)
    tn = lambda a_, b_: lax.dot_general(cast(a_), cast(b_), (((0,), (0,)), ((), ())), preferred_element_type=F32)

    cum = _cumsum_rows(lw)
    cum_last = cum[c - 1:c, :]
    p_in = jnp.exp(cum)
    p_ex = jnp.exp(cum - lw)
    p_inv = jnp.exp(-cum)
    d_end = jnp.exp(cum_last - cum)
    p_end = jnp.exp(cum_last)
    pt, rt = kk * p_ex, r * p_in
    kt, qt = k * p_inv, kka * p_inv
    kt_end, qt_end = k * d_end, kka * d_end

    ri = lax.broadcasted_iota(jnp.int32, (c, c), 0)
    ci = lax.broadcasted_iota(jnp.int32, (c, c), 1)
    strict = ri > ci
    incl = ri >= ci
    eye = (ri == ci).astype(F32)
    n_sq = max(int(np.ceil(np.log2(c))) - 1, 0)

    ys, s_new = [], []
    for h in range(n_heads):
        hs = slice(h * HEAD_DIM, (h + 1) * HEAD_DIM)
        lhs = jnp.concatenate([pt[:, hs], rt[:, hs]], axis=0)
        gq = nt(lhs, qt[:, hs])
        gk = nt(lhs, kt[:, hs])
        l_qp = jnp.where(strict, gq[:c], 0.0)
        l_kp = jnp.where(strict, gk[:c], 0.0)
        a_rq = jnp.where(incl, gq[c:], 0.0)
        a_rk = jnp.where(incl, gk[c:], 0.0)
        x = -l_qp
        t_inv = eye + x
        pw = x
        for _ in range(n_sq):
            pw = nn(pw, pw)
            t_inv = t_inv + nn(t_inv, pw)
        vh = v[:, hs]
        w_mat = nn(t_inv, pt[:, hs])
        u_v = nn(t_inv, nn(l_kp, vh))
        s_h = s_heads[h]
        ws = nt(jnp.concatenate([w_mat, rt[:, hs]], axis=0), s_h)
        u = u_v + ws[:c]
        ys.append(ws[c:] + nn(a_rk, vh) - nn(a_rq, u))
        upd = tn(jnp.concatenate([vh, -u], axis=0),
                 jnp.concatenate([kt_end[:, hs], qt_end[:, hs]], axis=0))
        s_new.append(s_h * p_end[:, hs] + upd)
    return jnp.concatenate(ys, axis=1), s_new


def _wkv_prompt_kernel(r_ref, lw_ref, k_ref, v_ref, kk_ref, kka_ref, y_ref, sT_ref, s_scr, *, chunks, tiles_per_seq):
    j = pl.program_id(0) % tiles_per_seq
    n_heads = s_scr.shape[0]

    @pl.when(j == 0)
    def _():
        s_scr[...] = jnp.zeros_like(s_scr)

    def body(ci, carry):
        rows = pl.ds(pl.multiple_of(ci * WKV_CHUNK, WKV_CHUNK), WKV_CHUNK)
        args = [ref[rows, :] for ref in (r_ref, lw_ref, k_ref, v_ref, kk_ref, kka_ref)]
        y, s_new = _wkv_chunk(*args, [s_scr[h] for h in range(n_heads)], BF16)
        y_ref[rows, :] = y
        for h in range(n_heads):
            s_scr[h] = s_new[h]
        return carry

    lax.fori_loop(0, chunks, body, 0)

    @pl.when(j == tiles_per_seq - 1)
    def _():
        sT_ref[0] = s_scr[...]


def _wkv_prompt(seqs, n_heads, arrays, ct):
    n, w = arrays[0].shape
    tiles_per_seq = n // seqs // ct
    row = pl.BlockSpec((ct, w), lambda i: (i, 0))
    st_shape = (seqs, n_heads, HEAD_DIM, HEAD_DIM)
    return pl.pallas_call(
        functools.partial(_wkv_prompt_kernel, chunks=ct // WKV_CHUNK, tiles_per_seq=tiles_per_seq),
        grid=(n // ct,),
        in_specs=[row] * 6,
        out_specs=[row, pl.BlockSpec((1,) + st_shape[1:], lambda i: (i // tiles_per_seq, 0, 0, 0))],
        out_shape=[jax.ShapeDtypeStruct((n, w), F32), jax.ShapeDtypeStruct(st_shape, F32)],
        scratch_shapes=[pltpu.VMEM(st_shape[1:], F32)],
        compiler_params=_cparams(1),
        name="wkv_prompt",
    )(*arrays)


def _wkv_sample_kernel(r_ref, lw_ref, k_ref, v_ref, kk_ref, kka_ref, s0_ref, y_ref, sT_ref, *, seqs_per_step):
    n_heads = s0_ref.shape[1]
    for s in range(seqs_per_step):
        args = [ref[:, s, 0, :] for ref in (r_ref, lw_ref, k_ref, v_ref, kk_ref, kka_ref)]
        y, s_new = _wkv_chunk(*args, [s0_ref[s, h] for h in range(n_heads)], F32)
        y_ref[:, s, 0, :] = y
        for h in range(n_heads):
            sT_ref[s, h] = s_new[h]


def _wkv_sample(t_len, seqs, n_heads, arrays, s0, seqs_per_step):
    w = arrays[0].shape[-1]
    arrays = [a.reshape(t_len, seqs, 1, w) for a in arrays]
    tok = pl.BlockSpec((t_len, seqs_per_step, 1, w), lambda i: (0, i, 0, 0))
    st = pl.BlockSpec((seqs_per_step, n_heads, HEAD_DIM, HEAD_DIM), lambda i: (i, 0, 0, 0))
    y, s_t = pl.pallas_call(
        functools.partial(_wkv_sample_kernel, seqs_per_step=seqs_per_step),
        grid=(seqs // seqs_per_step,),
        in_specs=[tok] * 6 + [st],
        out_specs=[tok, st],
        out_shape=[jax.ShapeDtypeStruct((t_len, seqs, 1, w), F32), jax.ShapeDtypeStruct(s0.shape, F32)],
        compiler_params=_cparams(1),
        name="wkv_sample",
    )(*arrays, s0)
    return y.reshape(t_len * seqs, w), s_t


def _attn_prompt_kernel(q_ref, kp_ref, kc_ref, vp_ref, vc_ref, o_ref, lse_ref, *, dil, n_steps, n_heads):
    blk = pl.program_id(2)
    qi = lax.broadcasted_iota(jnp.int32, (ATT_BLOCK, 2 * ATT_BLOCK), 0)
    ki = lax.broadcasted_iota(jnp.int32, (ATT_BLOCK, 2 * ATT_BLOCK), 1)
    steps = ATT_BLOCK + qi - ki
    valid = (steps >= 0) & (steps <= n_steps) & ((blk > 0) | (ki >= ATT_BLOCK))
    dist = (steps * dil).astype(F32)
    scale = HEAD_DIM ** -0.5
    q = q_ref[0]
    k = jnp.concatenate([kp_ref[0], kc_ref[0]], axis=0).astype(BF16)
    v = jnp.concatenate([vp_ref[0], vc_ref[0]], axis=0).astype(BF16)
    outs, lses = [], []
    for h in range(n_heads):
        hs = slice(h * HEAD_DIM, (h + 1) * HEAD_DIM)
        slope = float(2.0 ** (-8.0 * (h + 1) / n_heads))
        s = lax.dot_general(q[:, hs].astype(BF16), k[:, hs], (((1,), (1,)), ((), ())), preferred_element_type=F32)
        s = jnp.where(valid, s * scale - slope * dist, -jnp.inf)
        m = jnp.max(s, axis=-1, keepdims=True)
        e = jnp.exp(s - m)
        den = jnp.sum(e, axis=-1, keepdims=True)
        o = jnp.dot(e.astype(BF16), v[:, hs], preferred_element_type=F32) / den
        outs.append(o)
        lses.append(jnp.broadcast_to(m + jnp.log(den), o.shape))
    o_ref[0] = jnp.concatenate(outs, axis=1)
    lse_ref[0] = jnp.concatenate(lses, axis=1)


def _attn_prompt(q, k, v, seqs, dil, n_steps, n_heads):
    n, w = q.shape
    s_len = n // seqs
    view = lambda a: a.reshape(seqs, s_len // dil, dil * w)
    n_blk = s_len // (dil * ATT_BLOCK)
    cur = pl.BlockSpec((1, ATT_BLOCK, w), lambda b, r, i: (b, i, r))
    prev = pl.BlockSpec((1, ATT_BLOCK, w), lambda b, r, i: (b, jnp.maximum(i - 1, 0), r))
    o, lse = pl.pallas_call(
        functools.partial(_attn_prompt_kernel, dil=dil, n_steps=n_steps, n_heads=n_heads),
        grid=(seqs, dil, n_blk),
        in_specs=[cur, prev, cur, prev, cur],
        out_specs=[cur, cur],
        out_shape=[jax.ShapeDtypeStruct((seqs, s_len // dil, dil * w), F32)] * 2,
        compiler_params=_cparams(3),
        name=f"attn_prompt_d{dil}",
    )(view(q), view(k), view(k), view(v), view(v))
    return o.reshape(n, w), lse.reshape(n, w)


def _attn_sample_kernel(q_ref, kn_ref, vn_ref, kc_ref, vc_ref, o_ref, ko_ref, vo_ref, *, n_heads):
    t_len = q_ref.shape[0]
    win = kc_ref.shape[1]
    rows = n_heads * t_len
    q = q_ref[:, 0, 0, :]
    kn = kn_ref[:, 0, 0, :]
    vn = vn_ref[:, 0, 0, :]
    kc = kc_ref[0]
    vc = vc_ref[0]
    ko_ref[0, :win - t_len, :] = kc[t_len:]
    ko_ref[0, win - t_len:, :] = kn
    vo_ref[0, :win - t_len, :] = vc[t_len:]
    vo_ref[0, win - t_len:, :] = vn

    w = q.shape[1]
    qrep = jnp.concatenate([q] * n_heads, axis=0)
    rh = lax.broadcasted_iota(jnp.int32, (rows, w), 0) // t_len
    ch = lax.broadcasted_iota(jnp.int32, (rows, w), 1) // HEAD_DIM
    qexp = jnp.where(rh == ch, qrep, 0.0).astype(BF16)
    nt = lambda a_, b_: lax.dot_general(a_, b_, (((1,), (1,)), ((), ())), preferred_element_type=F32)
    scale = HEAD_DIM ** -0.5

    def weights(n_keys, key_pos0):
        ri = lax.broadcasted_iota(jnp.int32, (rows, n_keys), 0)
        ki = lax.broadcasted_iota(jnp.int32, (rows, n_keys), 1)
        dist = win + ri % t_len - (key_pos0 + ki)
        cnt = jnp.zeros((rows, n_keys), F32)
        for wdw, dil in zip(WINDOWS, DILATIONS):
            ok = (dist >= 0) & (dist % dil == 0) & (dist <= wdw)
            cnt = cnt + jnp.where(ok, 1.0, 0.0)
        slope = jnp.exp2(-8.0 * (ri // t_len + 1).astype(F32) / n_heads)
        return cnt, slope * dist.astype(F32)

    cnt_c, bias_c = weights(win, 0)
    cnt_n, bias_n = weights(t_len, win)
    s_c = jnp.where(cnt_c > 0, nt(qexp, kc.astype(BF16)) * scale - bias_c, -jnp.inf)
    s_n = jnp.where(cnt_n > 0, nt(qexp, kn.astype(BF16)) * scale - bias_n, -jnp.inf)
    m = jnp.maximum(jnp.max(s_c, axis=-1, keepdims=True), jnp.max(s_n, axis=-1, keepdims=True))
    e_c = cnt_c * jnp.exp(s_c - m)
    e_n = cnt_n * jnp.exp(s_n - m)
    den = jnp.sum(e_c, axis=-1, keepdims=True) + jnp.sum(e_n, axis=-1, keepdims=True)
    acc = (jnp.dot(e_c.astype(BF16), vc.astype(BF16), preferred_element_type=F32)
           + jnp.dot(e_n, vn, preferred_element_type=F32)) / den
    acc = jnp.where(rh == ch, acc, 0.0)
    out = acc[:t_len]
    for h in range(1, n_heads):
        out = out + acc[h * t_len:(h + 1) * t_len]
    o_ref[:, 0, 0, :] = out


def _attn_sample(q, k, v, kc, vc, t_len, seqs, n_heads):
    w = q.shape[-1]
    win = kc.shape[1]
    tok = pl.BlockSpec((t_len, 1, 1, w), lambda b: (0, b, 0, 0))
    cache = pl.BlockSpec((1, win, w), lambda b: (b, 0, 0))
    v4 = lambda a: a.reshape(t_len, seqs, 1, w)
    o, ko, vo = pl.pallas_call(
        functools.partial(_attn_sample_kernel, n_heads=n_heads),
        grid=(seqs,),
        in_specs=[tok, tok, tok, cache, cache],
        out_specs=[tok, cache, cache],
        out_shape=[jax.ShapeDtypeStruct((t_len, seqs, 1, w), F32)] + [jax.ShapeDtypeStruct(kc.shape, F32)] * 2,
        compiler_params=_cparams(1),
        name="attn_sample",
    )(v4(q), v4(k), v4(v), kc, vc)
    return o.reshape(t_len * seqs, w), ko, vo


def _post_kernel(*refs, n_branches, rw):
    (x_ref, y_ref, g_ref, bonus_ref), refs = refs[:4], refs[4:]
    if n_branches:
        o_refs, l_refs, refs = refs[:n_branches], refs[n_branches:2 * n_branches], refs[2 * n_branches:]
    else:
        at_ref, refs = refs[0], refs[1:]
    (lnw_ref, lnb_ref, gain_ref, wout_ref, nffn_ref, wr_hi_ref, wr_lo_ref, br_ref, bd_ref,
     x1_ref, hn_ref, comb_ref) = refs

    bd = bd_ref[...]
    y = y_ref[...]
    mean = _hilo_dot(y, bd) * (1.0 / HEAD_DIM)
    d = y - mean
    var = _hilo_dot(d * d, bd) * (1.0 / HEAD_DIM)
    yn = d * lax.rsqrt(var + GN_EPS) * lnw_ref[...] + lnb_ref[...]
    rw_out = (yn + bonus_ref[...]) * g_ref[...]

    if n_branches:
        ls = [l[...] for l in l_refs]
        m = functools.reduce(jnp.maximum, ls)
        es = [jnp.exp(l - m) for l in ls]
        at = sum(e * o[...] for e, o in zip(es, o_refs)) / sum(es)
    else:
        at = at_ref[...]
    at_out = _rms(at, gain_ref[...])

    x1 = (x_ref[...] + jnp.dot(rw_out.astype(BF16), wout_ref[:rw, :], preferred_element_type=F32)
          + jnp.dot(at_out.astype(BF16), wout_ref[rw:, :], preferred_element_type=F32))
    x1_ref[...] = x1
    hn = _rms(x1, nffn_ref[...])
    hn_ref[...] = hn.astype(BF16)

    hi, lo = _split2(hn)
    logits = (jnp.dot(hi, wr_hi_ref[...], preferred_element_type=F32)
              + jnp.dot(lo, wr_hi_ref[...], preferred_element_type=F32)
              + jnp.dot(hi, wr_lo_ref[...], preferred_element_type=F32)) + br_ref[...]
    lane = lax.broadcasted_iota(jnp.int32, logits.shape, 1)
    big = jnp.int32(ROUTER_LANES)
    first_max = lambda z: jnp.min(jnp.where(z == jnp.max(z, axis=-1, keepdims=True), lane, big), axis=-1, keepdims=True)
    gl = jnp.where(lane < N_GROUPS, logits, -jnp.inf)
    gsel = first_max(gl)
    gw = 1.0 / jnp.sum(jnp.exp(gl - jnp.max(gl, axis=-1, keepdims=True)), axis=-1, keepdims=True)
    e_lane = lane - EXPERT_LANE0
    in_group = (e_lane >= 0) & (e_lane < N_EXPERTS) & (e_lane // EXPERTS_PER_GROUP == gsel)
    el = jnp.where(in_group, logits, -jnp.inf)
    m1 = jnp.max(el, axis=-1, keepdims=True)
    i1 = first_max(el)
    el2 = jnp.where(lane == i1, -jnp.inf, el)
    m2 = jnp.max(el2, axis=-1, keepdims=True)
    i2 = first_max(el2)
    e2 = jnp.exp(m2 - m1)
    w1 = gw / (1.0 + e2)
    comb_ref[...] = jnp.where(lane == i1, w1, 0.0) + jnp.where(lane == i2, w1 * e2, 0.0)


def _post(x, y, g, bonus, attn, prm, rw, tm):
    n, d = x.shape
    row = lambda w: pl.BlockSpec((tm, w), lambda i: (i, 0))
    n_branches = 0 if not isinstance(attn, tuple) else len(attn[0])
    attn_arrays = list(attn[0]) + list(attn[1]) if n_branches else [attn]
    params = [prm[k] for k in ("ln_x_w", "ln_x_b", "attn_gain", "w_out", "norm_ffn", "wr_hi", "wr_lo", "b_router", "bd")]
    return pl.pallas_call(
        functools.partial(_post_kernel, n_branches=n_branches, rw=rw),
        grid=(n // tm,),
        in_specs=[row(d), row(rw), row(rw), row(rw)] + [row(a.shape[1]) for a in attn_arrays]
                 + [_full(p.shape) for p in params],
        out_specs=[row(d), row(d), row(ROUTER_LANES)],
        out_shape=[jax.ShapeDtypeStruct((n, d), F32), jax.ShapeDtypeStruct((n, d), BF16),
                   jax.ShapeDtypeStruct((n, ROUTER_LANES), F32)],
        compiler_params=_cparams(1),
        name="post",
    )(x, y, g, bonus, *attn_arrays, *params)


def _moe_kernel(hn_ref, comb_ref, wg_ref, wu_ref, wd_ref, o_ref):
    e = pl.program_id(1)

    @pl.when(e == 0)
    def _():
        o_ref[...] = jnp.zeros_like(o_ref)

    hn = hn_ref[...]
    gate = jnp.dot(hn, wg_ref[0], preferred_element_type=F32)
    up = jnp.dot(hn, wu_ref[0], preferred_element_type=F32)
    act = (gate * jax.nn.sigmoid(gate) * up).astype(BF16)
    out = jnp.dot(act, wd_ref[0], preferred_element_type=F32)
    comb = comb_ref[...]
    lane = lax.broadcasted_iota(jnp.int32, comb.shape, 1)
    c = jnp.sum(jnp.where(lane == e + EXPERT_LANE0, comb, 0.0), axis=-1, keepdims=True)
    o_ref[...] += c * out


def _moe(hn, comb, wg, wu, wd, tm):
    n, d = hn.shape
    n_exp, _, de = wg.shape
    return pl.pallas_call(
        _moe_kernel,
        grid=(n // tm, n_exp),
        in_specs=[pl.BlockSpec((tm, d), lambda i, e: (i, 0)),
                  pl.BlockSpec((tm, ROUTER_LANES), lambda i, e: (i, 0)),
                  pl.BlockSpec((1, d, de), lambda i, e: (e, 0, 0)),
                  pl.BlockSpec((1, d, de), lambda i, e: (e, 0, 0)),
                  pl.BlockSpec((1, de, d), lambda i, e: (e, 0, 0))],
        out_specs=pl.BlockSpec((tm, d), lambda i, e: (i, 0)),
        out_shape=jax.ShapeDtypeStruct((n, d), F32),
        compiler_params=_cparams(2),
        name="moe",
    )(hn, comb, wg, wu, wd)


def _tail_kernel(x1_ref, moe_ref, p_ref, nple_ref, wgate_ref, wple_ref, nfin_ref, y_ref):
    x2 = x1_ref[...] + moe_ref[...]
    gate = jax.nn.sigmoid(jnp.dot(_rms(x2, nple_ref[...]).astype(BF16), wgate_ref[...], preferred_element_type=F32))
    x3 = x2 + jnp.dot(p_ref[...].astype(BF16), wple_ref[...], preferred_element_type=F32) * gate
    y_ref[...] = _rms(x3, nfin_ref[...])


def _tail(x1, moe, p, prm, tm):
    n, d = x1.shape
    row = lambda w: pl.BlockSpec((tm, w), lambda i: (i, 0))
    params = [prm[k] for k in ("norm_ple", "w_ple_gate", "w_ple", "norm_final")]
    return pl.pallas_call(
        _tail_kernel,
        grid=(n // tm,),
        in_specs=[row(d), row(d), row(p.shape[1])] + [_full(a.shape) for a in params],
        out_specs=row(d),
        out_shape=jax.ShapeDtypeStruct((n, d), F32),
        compiler_params=_cparams(1),
        name="tail",
    )(x1, moe, p, *params)


def _layer_params(i, norm_mix, w_in, mu, w0, w2, a0, a2, g2, k_k, k_a, r_k, ln_x_w, ln_x_b, attn_gain, w_out,
                  norm_ffn, w_group, b_group, w_expert_router, b_expert_router, w_gate, w_up, w_down,
                  norm_ple, w_ple, w_ple_gate, norm_final):
    rw = w0.shape[1]
    d = w_in.shape[1]
    vec = lambda a: a[i].reshape(1, -1)
    lora_rows = DECAY_LORA + AAA_LORA + GATE_LORA

    def lora(w, lo):
        return jnp.zeros((lora_rows, rw), F32).at[lo:lo + w.shape[0]].set(w).astype(BF16)

    head = np.arange(rw) // HEAD_DIM
    w_router = jnp.zeros((d, ROUTER_LANES), F32)
    w_router = w_router.at[:, :N_GROUPS].set(w_group[i]).at[:, EXPERT_LANE0:EXPERT_LANE0 + N_EXPERTS].set(w_expert_router[i])
    wr_hi, wr_lo = _split2(w_router)
    b_router = jnp.zeros((1, ROUTER_LANES), F32)
    b_router = b_router.at[0, :N_GROUPS].set(b_group[i]).at[0, EXPERT_LANE0:EXPERT_LANE0 + N_EXPERTS].set(b_expert_router[i])
    return dict(
        norm_mix=vec(norm_mix), w_in=w_in[i].astype(BF16), mu=vec(mu), w0=vec(w0), a0=vec(a0),
        w2p=lora(w2[i], 0), a2p=lora(a2[i], DECAY_LORA), g2p=lora(g2[i], DECAY_LORA + AAA_LORA),
        k_k=vec(k_k), k_a=vec(k_a), r_k=vec(r_k), ln_x_w=vec(ln_x_w), ln_x_b=vec(ln_x_b),
        bd=jnp.asarray(head[:, None] == head[None, :], BF16),
        attn_gain=vec(attn_gain), w_out=w_out[i].astype(BF16), norm_ffn=vec(norm_ffn),
        wr_hi=wr_hi, wr_lo=wr_lo, b_router=b_router,
        w_gate=w_gate[i].astype(BF16), w_up=w_up[i].astype(BF16), w_down=w_down[i].astype(BF16),
        norm_ple=vec(norm_ple), w_ple=w_ple[i].astype(BF16), w_ple_gate=w_ple_gate[i].astype(BF16),
        norm_final=norm_final.reshape(1, -1),
    )


def _tile(n, want):
    t = min(n, want)
    assert n % t == 0, (n, t)
    return t


def _ffn_and_tail(x, y, g, bonus, attn, p, prm, rw):
    n = x.shape[0]
    x1, hn, comb = _post(x, y, g, bonus, attn, prm, rw, _tile(n, 256))
    moe = _moe(hn, comb, prm["w_gate"], prm["w_up"], prm["w_down"], _tile(n, 1024))
    return _tail(x1, moe, p, prm, _tile(n, 256))


def kernel(x_prompt, x_sample, state_wkv, state_shift, cache_k_win, cache_v_win, p_prompt, p_sample, norm_mix, w_in, mu, w0, w2, a0, a2, g2, k_k, k_a, r_k, ln_x_w, ln_x_b, attn_gain, w_out, norm_ffn, w_group, b_group, w_expert_router, b_expert_router, w_gate, w_up, w_down, norm_ple, w_ple, w_ple_gate, norm_final):
    depth = norm_mix.shape[0]
    assert depth == 1, "single-layer step"
    bp, s_len, d = x_prompt.shape
    bs, t_len, _ = x_sample.shape
    rw = w0.shape[1]
    n_rw_heads = rw // HEAD_DIM
    rw_cols = mu.shape[1]
    at_width = (w_in.shape[2] - rw_cols) // 3
    n_at_heads = at_width // HEAD_DIM
    win = cache_k_win.shape[2]
    assert s_len % (max(DILATIONS) * ATT_BLOCK) == 0 and s_len >= win and t_len == 8 and bs % 8 == 0
    assert all(wd // dl == ATT_BLOCK for wd, dl in zip(WINDOWS, DILATIONS)) and win >= max(WINDOWS)

    prm = _layer_params(0, norm_mix, w_in, mu, w0, w2, a0, a2, g2, k_k, k_a, r_k, ln_x_w, ln_x_b, attn_gain, w_out,
                        norm_ffn, w_group, b_group, w_expert_router, b_expert_router, w_gate, w_up, w_down,
                        norm_ple, w_ple, w_ple_gate, norm_final)

    n_p = bp * s_len
    xp = x_prompt.reshape(n_p, d)
    pr, q, k, v = _in_proj(xp, prm["norm_mix"], prm["w_in"], rw_cols, at_width, _tile(n_p, 256))
    tm = _tile(s_len, 512)
    r_, lw, km, vv, kk, kka, g, bonus = _rwkv_prep(pr, pr, prm, rw, tm, 1, s_len // tm)
    ct = _tile(s_len, 256)
    y, wkv_p = _wkv_prompt(bp, n_rw_heads, (r_, lw, km, vv, kk, kka), ct)
    branches = [_attn_prompt(q, k, v, bp, dl, wd // dl, n_at_heads) for wd, dl in zip(WINDOWS, DILATIONS)]
    attn = (tuple(b[0] for b in branches), tuple(b[1] for b in branches))
    y_prompt = _ffn_and_tail(xp, y, g, bonus, attn, p_prompt[0].reshape(n_p, -1), prm, rw).reshape(bp, s_len, d)
    shift_p = pr.reshape(bp, s_len, rw_cols)[:, -1]
    keep = min(win, s_len)
    kwin_p = k.reshape(bp, s_len, n_at_heads, HEAD_DIM)[:, s_len - keep:]
    vwin_p = v.reshape(bp, s_len, n_at_heads, HEAD_DIM)[:, s_len - keep:]

    n_s = bs * t_len
    xs = jnp.swapaxes(x_sample, 0, 1).reshape(n_s, d)
    ps = jnp.swapaxes(p_sample[0], 0, 1).reshape(n_s, -1)
    pr_s, q_s, k_s, v_s = _in_proj(xs, prm["norm_mix"], prm["w_in"], rw_cols, at_width, _tile(n_s, 256))
    r_, lw, km, vv, kk, kka, g, bonus = _rwkv_prep(pr_s, state_shift[0], prm, rw, n_s, bs, 1)
    y, wkv_s = _wkv_sample(t_len, bs, n_rw_heads, (r_, lw, km, vv, kk, kka), state_wkv[0], 4)
    at_s, kwin_s, vwin_s = _attn_sample(q_s, k_s, v_s, cache_k_win[0].reshape(bs, win, at_width),
                                        cache_v_win[0].reshape(bs, win, at_width), t_len, bs, n_at_heads)
    y_s = _ffn_and_tail(xs, y, g, bonus, at_s, ps, prm, rw)
    y_sample = jnp.swapaxes(y_s.reshape(t_len, bs, d), 0, 1)
    shift_s = pr_s[(t_len - 1) * bs:]
    cache_shape = cache_k_win.shape

    return (y_prompt, y_sample, wkv_p[None], shift_p[None], kwin_p[None], vwin_p[None],
            wkv_s[None], shift_s[None], kwin_s.reshape(cache_shape), vwin_s.reshape(cache_shape))
```

```python
import functools

import numpy as np
import jax
import jax.numpy as jnp
from jax import lax
from jax.experimental import pallas as pl
from jax.experimental.pallas import tpu as pltpu

F32 = jnp.float32
BF16 = jnp.bfloat16

HEAD_DIM = 64
DECAY_LORA = 32
AAA_LORA = 32
GATE_LORA = 64
GN_EPS = 64e-5
RMS_EPS = 1e-6
WINDOWS = (128, 512, 2048)
DILATIONS = (1, 4, 16)
ATT_BLOCK = 128
N_GROUPS = 4
EXPERTS_PER_GROUP = 8
N_EXPERTS = N_GROUPS * EXPERTS_PER_GROUP
ROUTER_LANES = 128
EXPERT_LANE0 = N_GROUPS
WKV_CHUNK = 64
VMEM_LIMIT = 56 * 1024 * 1024


def _cparams(n_axes):
    return pltpu.CompilerParams(dimension_semantics=("arbitrary",) * n_axes, vmem_limit_bytes=VMEM_LIMIT)


def _full(shape):
    nd = len(shape)
    return pl.BlockSpec(shape, lambda *_: (0,) * nd)


def _bdot(a, b):
    return jnp.dot(a.astype(BF16), b.astype(BF16), preferred_element_type=F32)


def _split2(a):
    hi = a.astype(BF16)
    lo = (a - hi.astype(F32)).astype(BF16)
    return hi, lo


def _hilo_dot(a, b_bf16):
    hi, lo = _split2(a)
    return (jnp.dot(hi, b_bf16, preferred_element_type=F32)
            + jnp.dot(lo, b_bf16, preferred_element_type=F32))


def _rms(x, g):
    return x * lax.rsqrt(jnp.mean(x * x, axis=-1, keepdims=True) + RMS_EPS) * g


def _in_proj_kernel(x_ref, g_ref, w_ref, pr_ref, q_ref, k_ref, v_ref, *, rw_cols, at_width):
    h = _rms(x_ref[...], g_ref[...]).astype(BF16)
    pr_ref[...] = jnp.dot(h, w_ref[:, :rw_cols], preferred_element_type=F32)
    for i, o_ref in enumerate((q_ref, k_ref, v_ref)):
        lo = rw_cols + i * at_width
        o_ref[...] = jnp.dot(h, w_ref[:, lo:lo + at_width], preferred_element_type=F32)


def _in_proj(x, g, w_bf16, rw_cols, at_width, tm):
    n, d = x.shape
    row = lambda w: pl.BlockSpec((tm, w), lambda i: (i, 0))
    return pl.pallas_call(
        functools.partial(_in_proj_kernel, rw_cols=rw_cols, at_width=at_width),
        grid=(n // tm,),
        in_specs=[row(d), _full(g.shape), _full(w_bf16.shape)],
        out_specs=[row(rw_cols), row(at_width), row(at_width), row(at_width)],
        out_shape=[jax.ShapeDtypeStruct((n, rw_cols), F32)] + [jax.ShapeDtypeStruct((n, at_width), F32)] * 3,
        compiler_params=_cparams(1),
        name="in_proj",
    )(x, g, w_bf16)


def _rwkv_prep_kernel(pr_ref, bnd_ref, mu_ref, w0_ref, w2_ref, a0_ref, a2_ref, g2_ref, kk_ref, ka_ref, rk_ref,
                      bd_ref, r_o, lw_o, k_o, v_o, kk_o, kka_o, g_o, bonus_o, *, rw, shift_rows, tiles_per_seq):
    pr = pr_ref[...]
    tm = pr.shape[0]
    if shift_rows == 1:
        first = pl.program_id(0) % tiles_per_seq == 0
        prev_row = jnp.where(first, 0.0, bnd_ref[7:8, :])
        row = lax.broadcasted_iota(jnp.int32, pr.shape, 0)
        prev = jnp.where(row == 0, prev_row, pltpu.roll(pr, 1, 0))
    else:
        prev = jnp.concatenate([bnd_ref[...], pr[:tm - shift_rows]], axis=0)
    xm = pr + (prev - pr) * mu_ref[...]
    xr, xk, xv, xl = xm[:, :rw], xm[:, rw:2 * rw], xm[:, 2 * rw:3 * rw], xm[:, 3 * rw:]
    bd = bd_ref[...]
    z = -(w0_ref[...] + _bdot(jnp.tanh(xl), w2_ref[...]))
    softplus = jnp.maximum(z, 0.0) + jnp.log(1.0 + jnp.exp(-jnp.abs(z)))
    lw = -jnp.exp(-softplus - 0.5)
    a = jax.nn.sigmoid(a0_ref[...] + _bdot(xl, a2_ref[...]))
    g = _bdot(jax.nn.sigmoid(xl), g2_ref[...])
    kk = xk * kk_ref[...]
    kk = kk / jnp.maximum(jnp.sqrt(_hilo_dot(kk * kk, bd)), 1e-12)
    kmod = xk * (1.0 + (a - 1.0) * ka_ref[...])
    bonus = _hilo_dot(xr * kmod * rk_ref[...], bd) * xv
    r_o[...] = xr
    lw_o[...] = lw
    k_o[...] = kmod
    v_o[...] = xv
    kk_o[...] = kk
    kka_o[...] = kk * a
    g_o[...] = g
    bonus_o[...] = bonus


def _rwkv_prep(pr, bnd, prm, rw, tm, shift_rows, tiles_per_seq):
    n, c = pr.shape
    row = lambda w: pl.BlockSpec((tm, w), lambda i: (i, 0))
    if shift_rows == 1:
        bnd_spec = pl.BlockSpec((8, c), lambda i: (jnp.maximum(i * (tm // 8) - 1, 0), 0))
    else:
        bnd_spec = _full(bnd.shape)
    params = [prm[k] for k in ("mu", "w0", "w2p", "a0", "a2p", "g2p", "k_k", "k_a", "r_k", "bd")]
    return pl.pallas_call(
        functools.partial(_rwkv_prep_kernel, rw=rw, shift_rows=shift_rows, tiles_per_seq=tiles_per_seq),
        grid=(n // tm,),
        in_specs=[row(c), bnd_spec] + [_full(p.shape) for p in params],
        out_specs=[row(rw)] * 8,
        out_shape=[jax.ShapeDtypeStruct((n, rw), F32)] * 8,
        compiler_params=_cparams(1),
        name="rwkv_prep",
    )(pr, bnd, *params)


def _cumsum_rows(x):
    n = x.shape[0]
    row = lax.broadcasted_iota(jnp.int32, x.shape, 0)
    s = 1
    while s < n:
        x = x + jnp.where(row >= s, pltpu.roll(x, s, 0), 0.0)
        s *= 2
    return x


def _lockstep(gens):
    results = [None] * len(gens)
    live = list(range(len(gens)))
    while live:
        still = []
        for i in live:
            try:
                next(gens[i])
                still.append(i)
            except StopIteration as stop:
                results[i] = stop.value
        live = still
    return results


def _wkv_head_chunk(pt, rt, kt, qt, kt_end, qt_end, vh, p_end, s_h, masks):
    strict, incl, eye = masks
    c = pt.shape[0]
    nt = lambda a_, b_: lax.dot_general(a_, b_, (((1,), (1,)), ((), ())), preferred_element_type=F32)
    nn = lambda a_, b_: jnp.dot(a_, b_, preferred_element_type=F32)
    tn = lambda a_, b_: lax.dot_general(a_, b_, (((0,), (0,)), ((), ())), preferred_element_type=F32)
    lhs = jnp.concatenate([pt, rt], axis=0)
    gq = nt(lhs, qt)
    gk = nt(lhs, kt)
    yield
    l_qp = jnp.where(strict, gq[:c], 0.0)
    l_kp = jnp.where(strict, gk[:c], 0.0)
    a_rq = jnp.where(incl, gq[c:], 0.0)
    a_rk = jnp.where(incl, gk[c:], 0.0)
    t_inv = eye - l_qp
    pw = -l_qp
    lv_av = nn(jnp.concatenate([l_kp, a_rk], axis=0), vh)
    for _ in range(int(np.log2(c)) - 1):
        pw = nn(pw, pw)
        yield
        t_new = nn(pw, t_inv)
        yield
        t_inv = t_inv + t_new
    w_mat = nn(t_inv, pt)
    u_v = nn(t_inv, lv_av[:c])
    yield
    ws = nt(jnp.concatenate([w_mat, rt], axis=0), s_h)
    yield
    u = u_v + ws[:c]
    au = nn(a_rq, u)
    upd = tn(jnp.concatenate([vh, -u], axis=0), jnp.concatenate([kt_end, qt_end], axis=0))
    yield
    return ws[c:] + lv_av[c:] - au, s_h * p_end + upd


PAIR = 2 * HEAD_DIM


def _pair_stack(z, first):
    return jnp.concatenate([jnp.where(first, z, 0.0), jnp.where(first, 0.0, z)], axis=0).astype(BF16)


def _wkv_pair_static(slabs, masks):
    r, lw, k, v, kk, kka = slabs
    first, strict, incl, eye = masks
    c = r.shape[0]
    n2 = 2 * c
    nt = lambda a_, b_: lax.dot_general(a_, b_, (((1,), (1,)), ((), ())), preferred_element_type=F32)
    nn = lambda a_, b_: jnp.dot(a_, b_, preferred_element_type=F32)
    cum = _cumsum_rows(lw)
    cum_last = cum[c - 1:c, :]
    d_end = jnp.exp(cum_last - cum)
    p_inv = jnp.exp(-cum)
    ps = _pair_stack(kk * jnp.exp(cum - lw), first)
    rs = _pair_stack(r * jnp.exp(cum), first)
    qs = _pair_stack(kka * p_inv, first)
    ks = _pair_stack(k * p_inv, first)
    vs = _pair_stack(v, first)
    ends = jnp.concatenate([_pair_stack(k * d_end, first), _pair_stack(kka * d_end, first)], axis=0)
    g = nt(jnp.concatenate([ps, rs], axis=0), jnp.concatenate([qs, ks], axis=0))
    yield
    l_qp = jnp.where(strict, g[:n2, :n2], 0.0)
    l_kp = jnp.where(strict, g[:n2, n2:], 0.0).astype(BF16)
    a_rq = jnp.where(incl, g[n2:, :n2], 0.0).astype(BF16)
    a_rk = jnp.where(incl, g[n2:, n2:], 0.0).astype(BF16)
    t_inv = eye - l_qp
    pw = (-l_qp).astype(BF16)
    lv_av = nn(jnp.concatenate([l_kp, a_rk], axis=0), vs)
    pw2 = nn(pw, pw)
    yield
    lv, av = lv_av[:n2].astype(BF16), lv_av[n2:]
    pw = pw2.astype(BF16)
    n_it = int(np.log2(c)) - 1
    for i in range(n_it):
        if i < n_it - 1:
            out = nn(pw, jnp.concatenate([pw, t_inv.astype(BF16)], axis=1))
            yield
            pw = out[:, :n2].astype(BF16)
            t_inv = t_inv + out[:, n2:]
        else:
            out = nn(pw, t_inv.astype(BF16))
            yield
            t_inv = t_inv + out
    tp = nn(t_inv.astype(BF16), jnp.concatenate([ps, lv], axis=1))
    yield
    return dict(w=tp[:, :PAIR].astype(BF16), u_v=tp[:, PAIR:], rs=rs, vs=vs, a_rq=a_rq, av=av, ends=ends,
                p_end=jnp.exp(cum_last))


def _wkv_pair_step(st, s_bd):
    nt = lambda a_, b_: lax.dot_general(a_, b_, (((1,), (1,)), ((), ())), preferred_element_type=F32)
    tn = lambda a_, b_: lax.dot_general(a_, b_, (((0,), (0,)), ((), ())), preferred_element_type=F32)
    n2 = st["w"].shape[0]
    ws = nt(jnp.concatenate([st["w"], st["rs"]], axis=0), s_bd.astype(BF16))
    yield
    u = st["u_v"] + ws[:n2]
    upd = tn(jnp.concatenate([st["vs"], (-u).astype(BF16)], axis=0), st["ends"])
    au = jnp.dot(st["a_rq"], u.astype(BF16), preferred_element_type=F32)
    yield
    y_bd = ws[n2:] + st["av"] - au
    return y_bd[:n2 // 2] + y_bd[n2 // 2:], s_bd * st["p_end"] + upd


def _wkv_prompt_kernel(r_ref, lw_ref, k_ref, v_ref, kk_ref, kka_ref, y_ref, sT_ref, s_scr, *, chunks, tiles_per_seq):
    j = pl.program_id(0) % tiles_per_seq
    n_pairs = s_scr.shape[0]
    c = WKV_CHUNK

    @pl.when(j == 0)
    def _():
        s_scr[...] = jnp.zeros_like(s_scr)

    lane = lax.broadcasted_iota(jnp.int32, (c, PAIR), 1)
    ri = lax.broadcasted_iota(jnp.int32, (2 * c, 2 * c), 0)
    ci = lax.broadcasted_iota(jnp.int32, (2 * c, 2 * c), 1)
    rt_, ct_ = ri % c, ci % c
    masks = (lane < HEAD_DIM, rt_ > ct_, rt_ >= ct_, (ri == ci).astype(F32))

    gens = []
    for ch in range(chunks):
        rows = slice(ch * c, (ch + 1) * c)
        for p in range(n_pairs):
            lanes = slice(p * PAIR, (p + 1) * PAIR)
            slabs = [ref[rows, lanes] for ref in (r_ref, lw_ref, k_ref, v_ref, kk_ref, kka_ref)]
            gens.append(_wkv_pair_static(slabs, masks))
    static = _lockstep(gens)
    states = [s_scr[p] for p in range(n_pairs)]
    for ch in range(chunks):
        outs = _lockstep([_wkv_pair_step(static[ch * n_pairs + p], states[p]) for p in range(n_pairs)])
        for p, (y, s_new) in enumerate(outs):
            states[p] = s_new
            y_ref[ch * c:(ch + 1) * c, p * PAIR:(p + 1) * PAIR] = y
    for p in range(n_pairs):
        s_scr[p] = states[p]

    @pl.when(j == tiles_per_seq - 1)
    def _():
        for p in range(n_pairs):
            sT_ref[0, 2 * p] = s_scr[p, :HEAD_DIM, :HEAD_DIM]
            sT_ref[0, 2 * p + 1] = s_scr[p, HEAD_DIM:, HEAD_DIM:]


def _wkv_prompt(seqs, n_heads, arrays, ct):
    n, w = arrays[0].shape
    tiles_per_seq = n // seqs // ct
    row = pl.BlockSpec((ct, w), lambda i: (i, 0))
    st_shape = (seqs, n_heads, HEAD_DIM, HEAD_DIM)
    return pl.pallas_call(
        functools.partial(_wkv_prompt_kernel, chunks=ct // WKV_CHUNK, tiles_per_seq=tiles_per_seq),
        grid=(n // ct,),
        in_specs=[row] * 6,
        out_specs=[row, pl.BlockSpec((1,) + st_shape[1:], lambda i: (i // tiles_per_seq, 0, 0, 0))],
        out_shape=[jax.ShapeDtypeStruct((n, w), F32), jax.ShapeDtypeStruct(st_shape, F32)],
        scratch_shapes=[pltpu.VMEM((n_heads // 2, PAIR, PAIR), F32)],
        compiler_params=_cparams(1),
        name="wkv_prompt",
    )(*arrays)


def _wkv_sample_kernel(r_ref, lw_ref, k_ref, v_ref, kk_ref, kka_ref, s0_ref, y_ref, sT_ref, *, seqs_per_step):
    n_heads = s0_ref.shape[1]
    c = r_ref.shape[0]
    ri = lax.broadcasted_iota(jnp.int32, (c, c), 0)
    ci = lax.broadcasted_iota(jnp.int32, (c, c), 1)
    masks = (ri > ci, ri >= ci, (ri == ci).astype(F32))
    gens = []
    for s in range(seqs_per_step):
        r, lw, k, v, kk, kka = [ref[:, s, 0, :] for ref in (r_ref, lw_ref, k_ref, v_ref, kk_ref, kka_ref)]
        cum = _cumsum_rows(lw)
        cum_last = cum[c - 1:c, :]
        p_inv = jnp.exp(-cum)
        d_end = jnp.exp(cum_last - cum)
        ops = (kk * jnp.exp(cum - lw), r * jnp.exp(cum), k * p_inv, kka * p_inv, k * d_end, kka * d_end, v,
               jnp.exp(cum_last))
        for h in range(n_heads):
            hs = slice(h * HEAD_DIM, (h + 1) * HEAD_DIM)
            gens.append(_wkv_head_chunk(*[z[:, hs] for z in ops], s0_ref[s, h], masks))
    outs = _lockstep(gens)
    for s in range(seqs_per_step):
        heads = outs[s * n_heads:(s + 1) * n_heads]
        y_ref[:, s, 0, :] = jnp.concatenate([y for y, _ in heads], axis=1)
        for h, (_, s_new) in enumerate(heads):
            sT_ref[s, h] = s_new


def _wkv_sample(t_len, seqs, n_heads, arrays, s0, seqs_per_step):
    w = arrays[0].shape[-1]
    arrays = [a.reshape(t_len, seqs, 1, w) for a in arrays]
    tok = pl.BlockSpec((t_len, seqs_per_step, 1, w), lambda i: (0, i, 0, 0))
    st = pl.BlockSpec((seqs_per_step, n_heads, HEAD_DIM, HEAD_DIM), lambda i: (i, 0, 0, 0))
    y, s_t = pl.pallas_call(
        functools.partial(_wkv_sample_kernel, seqs_per_step=seqs_per_step),
        grid=(seqs // seqs_per_step,),
        in_specs=[tok] * 6 + [st],
        out_specs=[tok, st],
        out_shape=[jax.ShapeDtypeStruct((t_len, seqs, 1, w), F32), jax.ShapeDtypeStruct(s0.shape, F32)],
        compiler_params=_cparams(1),
        name="wkv_sample",
    )(*arrays, s0)
    return y.reshape(t_len * seqs, w), s_t


def _attn_prompt_kernel(q_ref, kp_ref, kc_ref, vp_ref, vc_ref, o_ref, lse_ref, *, dil, n_steps, n_heads):
    blk = pl.program_id(2)
    qi = lax.broadcasted_iota(jnp.int32, (ATT_BLOCK, 2 * ATT_BLOCK), 0)
    ki = lax.broadcasted_iota(jnp.int32, (ATT_BLOCK, 2 * ATT_BLOCK), 1)
    steps = ATT_BLOCK + qi - ki
    valid = (steps >= 0) & (steps <= n_steps) & ((blk > 0) | (ki >= ATT_BLOCK))
    dist = (steps * dil).astype(F32)
    scale = HEAD_DIM ** -0.5
    q = q_ref[0]
    k = jnp.concatenate([kp_ref[0], kc_ref[0]], axis=0).astype(BF16)
    v = jnp.concatenate([vp_ref[0], vc_ref[0]], axis=0).astype(BF16)
    def head(h):
        hs = slice(h * HEAD_DIM, (h + 1) * HEAD_DIM)
        slope = float(2.0 ** (-8.0 * (h + 1) / n_heads))
        s = lax.dot_general(q[:, hs].astype(BF16), k[:, hs], (((1,), (1,)), ((), ())), preferred_element_type=F32)
        yield
        s = jnp.where(valid, s * scale - slope * dist, -jnp.inf)
        m = jnp.max(s, axis=-1, keepdims=True)
        e = jnp.exp(s - m)
        den = jnp.sum(e, axis=-1, keepdims=True)
        o = jnp.dot(e.astype(BF16), v[:, hs], preferred_element_type=F32)
        yield
        return o / den, jnp.broadcast_to(m + jnp.log(den), o.shape)

    res = _lockstep([head(h) for h in range(n_heads)])
    outs, lses = [r_[0] for r_ in res], [r_[1] for r_ in res]
    o_ref[0] = jnp.concatenate(outs, axis=1)
    lse_ref[0] = jnp.concatenate(lses, axis=1)


def _attn_prompt(q, k, v, seqs, dil, n_steps, n_heads):
    n, w = q.shape
    s_len = n // seqs
    view = lambda a: a.reshape(seqs, s_len // dil, dil * w)
    n_blk = s_len // (dil * ATT_BLOCK)
    cur = pl.BlockSpec((1, ATT_BLOCK, w), lambda b, r, i: (b, i, r))
    prev = pl.BlockSpec((1, ATT_BLOCK, w), lambda b, r, i: (b, jnp.maximum(i - 1, 0), r))
    o, lse = pl.pallas_call(
        functools.partial(_attn_prompt_kernel, dil=dil, n_steps=n_steps, n_heads=n_heads),
        grid=(seqs, dil, n_blk),
        in_specs=[cur, prev, cur, prev, cur],
        out_specs=[cur, cur],
        out_shape=[jax.ShapeDtypeStruct((seqs, s_len // dil, dil * w), F32)] * 2,
        compiler_params=_cparams(3),
        name=f"attn_prompt_d{dil}",
    )(view(q), view(k), view(k), view(v), view(v))
    return o.reshape(n, w), lse.reshape(n, w)


def _attn_sample_kernel(q_ref, kn_ref, vn_ref, kc_ref, vc_ref, o_ref, ko_ref, vo_ref, *, n_heads):
    t_len = q_ref.shape[0]
    win = kc_ref.shape[1]
    rows = n_heads * t_len
    q = q_ref[:, 0, 0, :]
    kn = kn_ref[:, 0, 0, :]
    vn = vn_ref[:, 0, 0, :]
    kc = kc_ref[0]
    vc = vc_ref[0]
    ko_ref[0, :win - t_len, :] = kc[t_len:]
    ko_ref[0, win - t_len:, :] = kn
    vo_ref[0, :win - t_len, :] = vc[t_len:]
    vo_ref[0, win - t_len:, :] = vn

    w = q.shape[1]
    qrep = jnp.concatenate([q] * n_heads, axis=0)
    rh = lax.broadcasted_iota(jnp.int32, (rows, w), 0) // t_len
    ch = lax.broadcasted_iota(jnp.int32, (rows, w), 1) // HEAD_DIM
    qexp = jnp.where(rh == ch, qrep, 0.0).astype(BF16)
    nt = lambda a_, b_: lax.dot_general(a_, b_, (((1,), (1,)), ((), ())), preferred_element_type=F32)
    scale = HEAD_DIM ** -0.5

    def weights(n_keys, key_pos0):
        ri = lax.broadcasted_iota(jnp.int32, (rows, n_keys), 0)
        ki = lax.broadcasted_iota(jnp.int32, (rows, n_keys), 1)
        dist = win + ri % t_len - (key_pos0 + ki)
        cnt = jnp.zeros((rows, n_keys), F32)
        for wdw, dil in zip(WINDOWS, DILATIONS):
            ok = (dist >= 0) & (dist % dil == 0) & (dist <= wdw)
            cnt = cnt + jnp.where(ok, 1.0, 0.0)
        slope = jnp.exp2(-8.0 * (ri // t_len + 1).astype(F32) / n_heads)
        return cnt, slope * dist.astype(F32)

    cnt_c, bias_c = weights(win, 0)
    cnt_n, bias_n = weights(t_len, win)
    s_c = jnp.where(cnt_c > 0, nt(qexp, kc.astype(BF16)) * scale - bias_c, -jnp.inf)
    s_n = jnp.where(cnt_n > 0, nt(qexp, kn.astype(BF16)) * scale - bias_n, -jnp.inf)
    m = jnp.maximum(jnp.max(s_c, axis=-1, keepdims=True), jnp.max(s_n, axis=-1, keepdims=True))
    e_c = cnt_c * jnp.exp(s_c - m)
    e_n = cnt_n * jnp.exp(s_n - m)
    den = jnp.sum(e_c, axis=-1, keepdims=True) + jnp.sum(e_n, axis=-1, keepdims=True)
    acc = (jnp.dot(e_c.astype(BF16), vc.astype(BF16), preferred_element_type=F32)
           + jnp.dot(e_n, vn, preferred_element_type=F32)) / den
    acc = jnp.where(rh == ch, acc, 0.0)
    out = acc[:t_len]
    for h in range(1, n_heads):
        out = out + acc[h * t_len:(h + 1) * t_len]
    o_ref[:, 0, 0, :] = out


def _attn_sample(q, k, v, kc, vc, t_len, seqs, n_heads):
    w = q.shape[-1]
    win = kc.shape[1]
    tok = pl.BlockSpec((t_len, 1, 1, w), lambda b: (0, b, 0, 0))
    cache = pl.BlockSpec((1, win, w), lambda b: (b, 0, 0))
    v4 = lambda a: a.reshape(t_len, seqs, 1, w)
    o, ko, vo = pl.pallas_call(
        functools.partial(_attn_sample_kernel, n_heads=n_heads),
        grid=(seqs,),
        in_specs=[tok, tok, tok, cache, cache],
        out_specs=[tok, cache, cache],
        out_shape=[jax.ShapeDtypeStruct((t_len, seqs, 1, w), F32)] + [jax.ShapeDtypeStruct(kc.shape, F32)] * 2,
        compiler_params=_cparams(1),
        name="attn_sample",
    )(v4(q), v4(k), v4(v), kc, vc)
    return o.reshape(t_len * seqs, w), ko, vo


def _post_kernel(*refs, n_branches, rw):
    (x_ref, y_ref, g_ref, bonus_ref), refs = refs[:4], refs[4:]
    if n_branches:
        o_refs, l_refs, refs = refs[:n_branches], refs[n_branches:2 * n_branches], refs[2 * n_branches:]
    else:
        at_ref, refs = refs[0], refs[1:]
    (lnw_ref, lnb_ref, gain_ref, wout_ref, nffn_ref, wr_hi_ref, wr_lo_ref, br_ref, bd_ref,
     x1_ref, hn_ref, comb_ref) = refs

    bd = bd_ref[...]
    y = y_ref[...]
    mean = _hilo_dot(y, bd) * (1.0 / HEAD_DIM)
    d = y - mean
    var = _hilo_dot(d * d, bd) * (1.0 / HEAD_DIM)
    yn = d * lax.rsqrt(var + GN_EPS) * lnw_ref[...] + lnb_ref[...]
    rw_out = (yn + bonus_ref[...]) * g_ref[...]

    if n_branches:
        ls = [l[...] for l in l_refs]
        m = functools.reduce(jnp.maximum, ls)
        es = [jnp.exp(l - m) for l in ls]
        at = sum(e * o[...] for e, o in zip(es, o_refs)) / sum(es)
    else:
        at = at_ref[...]
    at_out = _rms(at, gain_ref[...])

    x1 = (x_ref[...] + jnp.dot(rw_out.astype(BF16), wout_ref[:rw, :], preferred_element_type=F32)
          + jnp.dot(at_out.astype(BF16), wout_ref[rw:, :], preferred_element_type=F32))
    x1_ref[...] = x1
    hn = _rms(x1, nffn_ref[...])
    hn_ref[...] = hn.astype(BF16)

    hi, lo = _split2(hn)
    logits = (jnp.dot(hi, wr_hi_ref[...], preferred_element_type=F32)
              + jnp.dot(lo, wr_hi_ref[...], preferred_element_type=F32)
              + jnp.dot(hi, wr_lo_ref[...], preferred_element_type=F32)) + br_ref[...]
    lane = lax.broadcasted_iota(jnp.int32, logits.shape, 1)
    big = jnp.int32(ROUTER_LANES)
    first_max = lambda z: jnp.min(jnp.where(z == jnp.max(z, axis=-1, keepdims=True), lane, big), axis=-1, keepdims=True)
    gl = jnp.where(lane < N_GROUPS, logits, -jnp.inf)
    gsel = first_max(gl)
    gw = 1.0 / jnp.sum(jnp.exp(gl - jnp.max(gl, axis=-1, keepdims=True)), axis=-1, keepdims=True)
    e_lane = lane - EXPERT_LANE0
    in_group = (e_lane >= 0) & (e_lane < N_EXPERTS) & (e_lane // EXPERTS_PER_GROUP == gsel)
    el = jnp.where(in_group, logits, -jnp.inf)
    m1 = jnp.max(el, axis=-1, keepdims=True)
    i1 = first_max(el)
    el2 = jnp.where(lane == i1, -jnp.inf, el)
    m2 = jnp.max(el2, axis=-1, keepdims=True)
    i2 = first_max(el2)
    e2 = jnp.exp(m2 - m1)
    w1 = gw / (1.0 + e2)
    comb_ref[...] = jnp.where(lane == i1, w1, 0.0) + jnp.where(lane == i2, w1 * e2, 0.0)


def _post(x, y, g, bonus, attn, prm, rw, tm):
    n, d = x.shape
    row = lambda w: pl.BlockSpec((tm, w), lambda i: (i, 0))
    n_branches = 0 if not isinstance(attn, tuple) else len(attn[0])
    attn_arrays = list(attn[0]) + list(attn[1]) if n_branches else [attn]
    params = [prm[k] for k in ("ln_x_w", "ln_x_b", "attn_gain", "w_out", "norm_ffn", "wr_hi", "wr_lo", "b_router", "bd")]
    return pl.pallas_call(
        functools.partial(_post_kernel, n_branches=n_branches, rw=rw),
        grid=(n // tm,),
        in_specs=[row(d), row(rw), row(rw), row(rw)] + [row(a.shape[1]) for a in attn_arrays]
                 + [_full(p.shape) for p in params],
        out_specs=[row(d), row(d), row(ROUTER_LANES)],
        out_shape=[jax.ShapeDtypeStruct((n, d), F32), jax.ShapeDtypeStruct((n, d), BF16),
                   jax.ShapeDtypeStruct((n, ROUTER_LANES), F32)],
        compiler_params=_cparams(1),
        name="post",
    )(x, y, g, bonus, *attn_arrays, *params)


def _moe_kernel(hn_ref, comb_ref, wg_ref, wu_ref, wd_ref, o_ref):
    e = pl.program_id(1)

    @pl.when(e == 0)
    def _():
        o_ref[...] = jnp.zeros_like(o_ref)

    hn = hn_ref[...]
    gate = jnp.dot(hn, wg_ref[0], preferred_element_type=F32)
    up = jnp.dot(hn, wu_ref[0], preferred_element_type=F32)
    act = (gate * jax.nn.sigmoid(gate) * up).astype(BF16)
    out = jnp.dot(act, wd_ref[0], preferred_element_type=F32)
    comb = comb_ref[...]
    lane = lax.broadcasted_iota(jnp.int32, comb.shape, 1)
    c = jnp.sum(jnp.where(lane == e + EXPERT_LANE0, comb, 0.0), axis=-1, keepdims=True)
    o_ref[...] += c * out


def _moe(hn, comb, wg, wu, wd, tm):
    n, d = hn.shape
    n_exp, _, de = wg.shape
    return pl.pallas_call(
        _moe_kernel,
        grid=(n // tm, n_exp),
        in_specs=[pl.BlockSpec((tm, d), lambda i, e: (i, 0)),
                  pl.BlockSpec((tm, ROUTER_LANES), lambda i, e: (i, 0)),
                  pl.BlockSpec((1, d, de), lambda i, e: (e, 0, 0)),
                  pl.BlockSpec((1, d, de), lambda i, e: (e, 0, 0)),
                  pl.BlockSpec((1, de, d), lambda i, e: (e, 0, 0))],
        out_specs=pl.BlockSpec((tm, d), lambda i, e: (i, 0)),
        out_shape=jax.ShapeDtypeStruct((n, d), F32),
        compiler_params=_cparams(2),
        name="moe",
    )(hn, comb, wg, wu, wd)


def _tail_kernel(x1_ref, moe_ref, p_ref, nple_ref, wgate_ref, wple_ref, nfin_ref, y_ref):
    x2 = x1_ref[...] + moe_ref[...]
    gate = jax.nn.sigmoid(jnp.dot(_rms(x2, nple_ref[...]).astype(BF16), wgate_ref[...], preferred_element_type=F32))
    x3 = x2 + jnp.dot(p_ref[...].astype(BF16), wple_ref[...], preferred_element_type=F32) * gate
    y_ref[...] = _rms(x3, nfin_ref[...])


def _tail(x1, moe, p, prm, tm):
    n, d = x1.shape
    row = lambda w: pl.BlockSpec((tm, w), lambda i: (i, 0))
    params = [prm[k] for k in ("norm_ple", "w_ple_gate", "w_ple", "norm_final")]
    return pl.pallas_call(
        _tail_kernel,
        grid=(n // tm,),
        in_specs=[row(d), row(d), row(p.shape[1])] + [_full(a.shape) for a in params],
        out_specs=row(d),
        out_shape=jax.ShapeDtypeStruct((n, d), F32),
        compiler_params=_cparams(1),
        name="tail",
    )(x1, moe, p, *params)


def _layer_params(i, norm_mix, w_in, mu, w0, w2, a0, a2, g2, k_k, k_a, r_k, ln_x_w, ln_x_b, attn_gain, w_out,
                  norm_ffn, w_group, b_group, w_expert_router, b_expert_router, w_gate, w_up, w_down,
                  norm_ple, w_ple, w_ple_gate, norm_final):
    rw = w0.shape[1]
    d = w_in.shape[1]
    vec = lambda a: a[i].reshape(1, -1)
    lora_rows = DECAY_LORA + AAA_LORA + GATE_LORA

    def lora(w, lo):
        return jnp.zeros((lora_rows, rw), F32).at[lo:lo + w.shape[0]].set(w).astype(BF16)

    head = np.arange(rw) // HEAD_DIM
    w_router = jnp.zeros((d, ROUTER_LANES), F32)
    w_router = w_router.at[:, :N_GROUPS].set(w_group[i]).at[:, EXPERT_LANE0:EXPERT_LANE0 + N_EXPERTS].set(w_expert_router[i])
    wr_hi, wr_lo = _split2(w_router)
    b_router = jnp.zeros((1, ROUTER_LANES), F32)
    b_router = b_router.at[0, :N_GROUPS].set(b_group[i]).at[0, EXPERT_LANE0:EXPERT_LANE0 + N_EXPERTS].set(b_expert_router[i])
    return dict(
        norm_mix=vec(norm_mix), w_in=w_in[i].astype(BF16), mu=vec(mu), w0=vec(w0), a0=vec(a0),
        w2p=lora(w2[i], 0), a2p=lora(a2[i], DECAY_LORA), g2p=lora(g2[i], DECAY_LORA + AAA_LORA),
        k_k=vec(k_k), k_a=vec(k_a), r_k=vec(r_k), ln_x_w=vec(ln_x_w), ln_x_b=vec(ln_x_b),
        bd=jnp.asarray(head[:, None] == head[None, :], BF16),
        attn_gain=vec(attn_gain), w_out=w_out[i].astype(BF16), norm_ffn=vec(norm_ffn),
        wr_hi=wr_hi, wr_lo=wr_lo, b_router=b_router,
        w_gate=w_gate[i].astype(BF16), w_up=w_up[i].astype(BF16), w_down=w_down[i].astype(BF16),
        norm_ple=vec(norm_ple), w_ple=w_ple[i].astype(BF16), w_ple_gate=w_ple_gate[i].astype(BF16),
        norm_final=norm_final.reshape(1, -1),
    )


def _tile(n, want):
    t = min(n, want)
    assert n % t == 0, (n, t)
    return t


def _ffn_and_tail(x, y, g, bonus, attn, p, prm, rw):
    n = x.shape[0]
    x1, hn, comb = _post(x, y, g, bonus, attn, prm, rw, _tile(n, 256))
    moe = _moe(hn, comb, prm["w_gate"], prm["w_up"], prm["w_down"], _tile(n, 1024))
    return _tail(x1, moe, p, prm, _tile(n, 256))


def kernel(x_prompt, x_sample, state_wkv, state_shift, cache_k_win, cache_v_win, p_prompt, p_sample, norm_mix, w_in, mu, w0, w2, a0, a2, g2, k_k, k_a, r_k, ln_x_w, ln_x_b, attn_gain, w_out, norm_ffn, w_group, b_group, w_expert_router, b_expert_router, w_gate, w_up, w_down, norm_ple, w_ple, w_ple_gate, norm_final):
    depth = norm_mix.shape[0]
    assert depth == 1, "single-layer step"
    bp, s_len, d = x_prompt.shape
    bs, t_len, _ = x_sample.shape
    rw = w0.shape[1]
    n_rw_heads = rw // HEAD_DIM
    rw_cols = mu.shape[1]
    at_width = (w_in.shape[2] - rw_cols) // 3
    n_at_heads = at_width // HEAD_DIM
    win = cache_k_win.shape[2]
    assert s_len % (max(DILATIONS) * ATT_BLOCK) == 0 and s_len >= win and t_len == 8 and bs % 8 == 0
    assert all(wd // dl == ATT_BLOCK for wd, dl in zip(WINDOWS, DILATIONS)) and win >= max(WINDOWS)

    prm = _layer_params(0, norm_mix, w_in, mu, w0, w2, a0, a2, g2, k_k, k_a, r_k, ln_x_w, ln_x_b, attn_gain, w_out,
                        norm_ffn, w_group, b_group, w_expert_router, b_expert_router, w_gate, w_up, w_down,
                        norm_ple, w_ple, w_ple_gate, norm_final)

    n_p = bp * s_len
    xp = x_prompt.reshape(n_p, d)
    pr, q, k, v = _in_proj(xp, prm["norm_mix"], prm["w_in"], rw_cols, at_width, _tile(n_p, 256))
    tm = _tile(s_len, 512)
    r_, lw, km, vv, kk, kka, g, bonus = _rwkv_prep(pr, pr, prm, rw, tm, 1, s_len // tm)
    ct = _tile(s_len, 256)
    y, wkv_p = _wkv_prompt(bp, n_rw_heads, (r_, lw, km, vv, kk, kka), ct)
    branches = [_attn_prompt(q, k, v, bp, dl, wd // dl, n_at_heads) for wd, dl in zip(WINDOWS, DILATIONS)]
    attn = (tuple(b[0] for b in branches), tuple(b[1] for b in branches))
    y_prompt = _ffn_and_tail(xp, y, g, bonus, attn, p_prompt[0].reshape(n_p, -1), prm, rw).reshape(bp, s_len, d)
    shift_p = pr.reshape(bp, s_len, rw_cols)[:, -1]
    keep = min(win, s_len)
    kwin_p = k.reshape(bp, s_len, n_at_heads, HEAD_DIM)[:, s_len - keep:]
    vwin_p = v.reshape(bp, s_len, n_at_heads, HEAD_DIM)[:, s_len - keep:]

    n_s = bs * t_len
    xs = jnp.swapaxes(x_sample, 0, 1).reshape(n_s, d)
    ps = jnp.swapaxes(p_sample[0], 0, 1).reshape(n_s, -1)
    pr_s, q_s, k_s, v_s = _in_proj(xs, prm["norm_mix"], prm["w_in"], rw_cols, at_width, _tile(n_s, 256))
    r_, lw, km, vv, kk, kka, g, bonus = _rwkv_prep(pr_s, state_shift[0], prm, rw, n_s, bs, 1)
    y, wkv_s = _wkv_sample(t_len, bs, n_rw_heads, (r_, lw, km, vv, kk, kka), state_wkv[0], 4)
    at_s, kwin_s, vwin_s = _attn_sample(q_s, k_s, v_s, cache_k_win[0].reshape(bs, win, at_width),
                                        cache_v_win[0].reshape(bs, win, at_width), t_len, bs, n_at_heads)
    y_s = _ffn_and_tail(xs, y, g, bonus, at_s, ps, prm, rw)
    y_sample = jnp.swapaxes(y_s.reshape(t_len, bs, d), 0, 1)
    shift_s = pr_s[(t_len - 1) * bs:]
    cache_shape = cache_k_win.shape

    return (y_prompt, y_sample, wkv_p[None], shift_p[None], kwin_p[None], vwin_p[None],
            wkv_s[None], shift_s[None], kwin_s.reshape(cache_shape), vwin_s.reshape(cache_shape))
```

```python
import functools

import numpy as np
import jax
import jax.numpy as jnp
from jax import lax
from jax.experimental import pallas as pl
from jax.experimental.pallas import tpu as pltpu

F32 = jnp.float32
BF16 = jnp.bfloat16

LANES = 128
HEAD_DIM = 64
PAIR = 2 * HEAD_DIM
DECAY_LORA = 32
AAA_LORA = 32
GATE_LORA = 64
GN_EPS = 64e-5
RMS_EPS = 1e-6
WINDOWS = (128, 512, 2048)
DILATIONS = (1, 4, 16)
ATT_BLOCK = 128
N_GROUPS = 4
EXPERTS_PER_GROUP = 8
N_EXPERTS = N_GROUPS * EXPERTS_PER_GROUP
ROUTER_LANES = 128
EXPERT_LANE0 = N_GROUPS
WKV_CHUNK = 64
VMEM_LIMIT = 56 * 1024 * 1024


def _cparams(n_axes):
    return pltpu.CompilerParams(dimension_semantics=("arbitrary",) * n_axes, vmem_limit_bytes=VMEM_LIMIT)


def _full(shape):
    nd = len(shape)
    return pl.BlockSpec(shape, lambda *_: (0,) * nd)


def _bdot(a, b):
    return jnp.dot(a.astype(BF16), b.astype(BF16), preferred_element_type=F32)


def _split2(a):
    hi = a.astype(BF16)
    lo = (a - hi.astype(F32)).astype(BF16)
    return hi, lo


def _hilo_dot(a, b_bf16):
    hi, lo = _split2(a)
    return (jnp.dot(hi, b_bf16, preferred_element_type=F32)
            + jnp.dot(lo, b_bf16, preferred_element_type=F32))


def _rms(x, g):
    return x * lax.rsqrt(jnp.mean(x * x, axis=-1, keepdims=True) + RMS_EPS) * g


def _in_proj_kernel(x_ref, g_ref, w_ref, pr_ref, q_ref, k_ref, v_ref, *, rw_cols, at_width, slabs):
    h = _rms(x_ref[...], g_ref[...]).astype(BF16)
    pr_ref[...] = jnp.dot(h, w_ref[:, :rw_cols], preferred_element_type=F32)
    for i, o_ref in enumerate((q_ref, k_ref, v_ref)):
        lo = rw_cols + i * at_width
        z = jnp.dot(h, w_ref[:, lo:lo + at_width], preferred_element_type=F32)
        if slabs:
            for s in range(at_width // PAIR):
                o_ref[s] = z[:, s * PAIR:(s + 1) * PAIR]
        else:
            o_ref[...] = z


def _in_proj(x, g, w_bf16, rw_cols, at_width, tm, slabs):
    n, d = x.shape
    row = lambda w: pl.BlockSpec((tm, w), lambda i: (i, 0))
    if slabs:
        qkv_spec = pl.BlockSpec((at_width // PAIR, tm, PAIR), lambda i: (0, i, 0))
        qkv_shape = jax.ShapeDtypeStruct((at_width // PAIR, n, PAIR), F32)
    else:
        qkv_spec, qkv_shape = row(at_width), jax.ShapeDtypeStruct((n, at_width), F32)
    return pl.pallas_call(
        functools.partial(_in_proj_kernel, rw_cols=rw_cols, at_width=at_width, slabs=slabs),
        grid=(n // tm,),
        in_specs=[row(d), _full(g.shape), _full(w_bf16.shape)],
        out_specs=[row(rw_cols)] + [qkv_spec] * 3,
        out_shape=[jax.ShapeDtypeStruct((n, rw_cols), F32)] + [qkv_shape] * 3,
        compiler_params=_cparams(1),
        name="in_proj",
    )(x, g, w_bf16)


def _rwkv_prep_kernel(pr_ref, bnd_ref, mu_ref, w0_ref, w2_ref, a0_ref, a2_ref, g2_ref, kk_ref, ka_ref, rk_ref,
                      bd_ref, r_o, lw_o, k_o, v_o, kk_o, kka_o, g_o, bonus_o, *, rw, shift_rows, tiles_per_seq):
    pr = pr_ref[...]
    tm = pr.shape[0]
    if shift_rows == 1:
        first = pl.program_id(0) % tiles_per_seq == 0
        prev_row = jnp.where(first, 0.0, bnd_ref[7:8, :])
        row = lax.broadcasted_iota(jnp.int32, pr.shape, 0)
        prev = jnp.where(row == 0, prev_row, pltpu.roll(pr, 1, 0))
    else:
        prev = jnp.concatenate([bnd_ref[...], pr[:tm - shift_rows]], axis=0)
    xm = pr + (prev - pr) * mu_ref[...]
    xr, xk, xv, xl = xm[:, :rw], xm[:, rw:2 * rw], xm[:, 2 * rw:3 * rw], xm[:, 3 * rw:]
    bd = bd_ref[...]
    z = -(w0_ref[...] + _bdot(jnp.tanh(xl), w2_ref[...]))
    softplus = jnp.maximum(z, 0.0) + jnp.log(1.0 + jnp.exp(-jnp.abs(z)))
    lw = -jnp.exp(-softplus - 0.5)
    a = jax.nn.sigmoid(a0_ref[...] + _bdot(xl, a2_ref[...]))
    g = _bdot(jax.nn.sigmoid(xl), g2_ref[...])
    kk = xk * kk_ref[...]
    kk = kk / jnp.maximum(jnp.sqrt(_hilo_dot(kk * kk, bd)), 1e-12)
    kmod = xk * (1.0 + (a - 1.0) * ka_ref[...])
    bonus = _hilo_dot(xr * kmod * rk_ref[...], bd) * xv
    r_o[...] = xr
    lw_o[...] = lw
    k_o[...] = kmod
    v_o[...] = xv
    kk_o[...] = kk
    kka_o[...] = kk * a
    g_o[...] = g
    bonus_o[...] = bonus


def _rwkv_prep(pr, bnd, prm, rw, tm, shift_rows, tiles_per_seq):
    n, c = pr.shape
    row = lambda w: pl.BlockSpec((tm, w), lambda i: (i, 0))
    if shift_rows == 1:
        bnd_spec = pl.BlockSpec((8, c), lambda i: (jnp.maximum(i * (tm // 8) - 1, 0), 0))
    else:
        bnd_spec = _full(bnd.shape)
    params = [prm[k] for k in ("mu", "w0", "w2p", "a0", "a2p", "g2p", "k_k", "k_a", "r_k", "bd")]
    return pl.pallas_call(
        functools.partial(_rwkv_prep_kernel, rw=rw, shift_rows=shift_rows, tiles_per_seq=tiles_per_seq),
        grid=(n // tm,),
        in_specs=[row(c), bnd_spec] + [_full(p.shape) for p in params],
        out_specs=[row(rw)] * 8,
        out_shape=[jax.ShapeDtypeStruct((n, rw), F32)] * 8,
        compiler_params=_cparams(1),
        name="rwkv_prep",
    )(pr, bnd, *params)


def _cumsum_rows(x):
    n = x.shape[0]
    row = lax.broadcasted_iota(jnp.int32, x.shape, 0)
    s = 1
    while s < n:
        x = x + jnp.where(row >= s, pltpu.roll(x, s, 0), 0.0)
        s *= 2
    return x


def _lockstep(gens):
    results = [None] * len(gens)
    live = list(range(len(gens)))
    while live:
        still = []
        for i in live:
            try:
                next(gens[i])
                still.append(i)
            except StopIteration as stop:
                results[i] = stop.value
        live = still
    return results


def _wkv_head_chunk(pt, rt, kt, qt, kt_end, qt_end, vh, p_end, s_h, masks):
    strict, incl, eye = masks
    c = pt.shape[0]
    nt = lambda a_, b_: lax.dot_general(a_, b_, (((1,), (1,)), ((), ())), preferred_element_type=F32)
    nn = lambda a_, b_: jnp.dot(a_, b_, preferred_element_type=F32)
    tn = lambda a_, b_: lax.dot_general(a_, b_, (((0,), (0,)), ((), ())), preferred_element_type=F32)
    lhs = jnp.concatenate([pt, rt], axis=0)
    gq = nt(lhs, qt)
    gk = nt(lhs, kt)
    yield
    l_qp = jnp.where(strict, gq[:c], 0.0)
    l_kp = jnp.where(strict, gk[:c], 0.0)
    a_rq = jnp.where(incl, gq[c:], 0.0)
    a_rk = jnp.where(incl, gk[c:], 0.0)
    t_inv = eye - l_qp
    pw = -l_qp
    lv_av = nn(jnp.concatenate([l_kp, a_rk], axis=0), vh)
    for _ in range(int(np.log2(c)) - 1):
        pw = nn(pw, pw)
        yield
        t_new = nn(pw, t_inv)
        yield
        t_inv = t_inv + t_new
    w_mat = nn(t_inv, pt)
    u_v = nn(t_inv, lv_av[:c])
    yield
    ws = nt(jnp.concatenate([w_mat, rt], axis=0), s_h)
    yield
    u = u_v + ws[:c]
    au = nn(a_rq, u)
    upd = tn(jnp.concatenate([vh, -u], axis=0), jnp.concatenate([kt_end, qt_end], axis=0))
    yield
    return ws[c:] + lv_av[c:] - au, s_h * p_end + upd


def _pair_stack(z, first):
    return jnp.concatenate([jnp.where(first, z, 0.0), jnp.where(first, 0.0, z)], axis=0).astype(BF16)


def _wkv_pair_static(slabs, masks):
    r, lw, k, v, kk, kka = slabs
    first, strict, incl, eye = masks
    c = r.shape[0]
    n2 = 2 * c
    nt = lambda a_, b_: lax.dot_general(a_, b_, (((1,), (1,)), ((), ())), preferred_element_type=F32)
    nn = lambda a_, b_: jnp.dot(a_, b_, preferred_element_type=F32)
    cum = _cumsum_rows(lw)
    cum_last = cum[c - 1:c, :]
    d_end = jnp.exp(cum_last - cum)
    p_inv = jnp.exp(-cum)
    ps = _pair_stack(kk * jnp.exp(cum - lw), first)
    rs = _pair_stack(r * jnp.exp(cum), first)
    qs = _pair_stack(kka * p_inv, first)
    ks = _pair_stack(k * p_inv, first)
    vs = _pair_stack(v, first)
    ends = jnp.concatenate([_pair_stack(k * d_end, first), _pair_stack(kka * d_end, first)], axis=0)
    g = nt(jnp.concatenate([ps, rs], axis=0), jnp.concatenate([qs, ks], axis=0))
    yield
    l_qp = jnp.where(strict, g[:n2, :n2], 0.0)
    l_kp = jnp.where(strict, g[:n2, n2:], 0.0).astype(BF16)
    a_rq = jnp.where(incl, g[n2:, :n2], 0.0).astype(BF16)
    a_rk = jnp.where(incl, g[n2:, n2:], 0.0).astype(BF16)
    t_inv = eye - l_qp
    pw = (-l_qp).astype(BF16)
    lv_av = nn(jnp.concatenate([l_kp, a_rk], axis=0), vs)
    pw2 = nn(pw, pw)
    yield
    lv, av = lv_av[:n2].astype(BF16), lv_av[n2:]
    pw = pw2.astype(BF16)
    n_it = int(np.log2(c)) - 1
    for i in range(n_it):
        if i < n_it - 1:
            out = nn(pw, jnp.concatenate([pw, t_inv.astype(BF16)], axis=1))
            yield
            pw = out[:, :n2].astype(BF16)
            t_inv = t_inv + out[:, n2:]
        else:
            out = nn(pw, t_inv.astype(BF16))
            yield
            t_inv = t_inv + out
    tp = nn(t_inv.astype(BF16), jnp.concatenate([ps, lv], axis=1))
    yield
    return dict(w=tp[:, :PAIR].astype(BF16), u_v=tp[:, PAIR:], rs=rs, vs=vs, a_rq=a_rq, av=av, ends=ends,
                p_end=jnp.exp(cum_last))


def _wkv_pair_step(st, s_bd):
    nt = lambda a_, b_: lax.dot_general(a_, b_, (((1,), (1,)), ((), ())), preferred_element_type=F32)
    tn = lambda a_, b_: lax.dot_general(a_, b_, (((0,), (0,)), ((), ())), preferred_element_type=F32)
    n2 = st["w"].shape[0]
    ws = nt(jnp.concatenate([st["w"], st["rs"]], axis=0), s_bd.astype(BF16))
    yield
    u = st["u_v"] + ws[:n2]
    upd = tn(jnp.concatenate([st["vs"], (-u).astype(BF16)], axis=0), st["ends"])
    au = jnp.dot(st["a_rq"], u.astype(BF16), preferred_element_type=F32)
    yield
    y_bd = ws[n2:] + st["av"] - au
    return y_bd[:n2 // 2] + y_bd[n2 // 2:], s_bd * st["p_end"] + upd


def _wkv_prompt_kernel(r_ref, lw_ref, k_ref, v_ref, kk_ref, kka_ref, y_ref, sT_ref, s_scr, *, chunks, tiles_per_seq):
    j = pl.program_id(0) % tiles_per_seq
    n_pairs = s_scr.shape[0]
    c = WKV_CHUNK

    @pl.when(j == 0)
    def _():
        s_scr[...] = jnp.zeros_like(s_scr)

    lane = lax.broadcasted_iota(jnp.int32, (c, PAIR), 1)
    ri = lax.broadcasted_iota(jnp.int32, (2 * c, 2 * c), 0)
    ci = lax.broadcasted_iota(jnp.int32, (2 * c, 2 * c), 1)
    rt_, ct_ = ri % c, ci % c
    masks = (lane < HEAD_DIM, rt_ > ct_, rt_ >= ct_, (ri == ci).astype(F32))

    gens = []
    for ch in range(chunks):
        rows = slice(ch * c, (ch + 1) * c)
        for p in range(n_pairs):
            lanes = slice(p * PAIR, (p + 1) * PAIR)
            slabs = [ref[rows, lanes] for ref in (r_ref, lw_ref, k_ref, v_ref, kk_ref, kka_ref)]
            gens.append(_wkv_pair_static(slabs, masks))
    static = _lockstep(gens)
    states = [s_scr[p] for p in range(n_pairs)]
    for ch in range(chunks):
        outs = _lockstep([_wkv_pair_step(static[ch * n_pairs + p], states[p]) for p in range(n_pairs)])
        for p, (y, s_new) in enumerate(outs):
            states[p] = s_new
            y_ref[ch * c:(ch + 1) * c, p * PAIR:(p + 1) * PAIR] = y
    for p in range(n_pairs):
        s_scr[p] = states[p]

    @pl.when(j == tiles_per_seq - 1)
    def _():
        for p in range(n_pairs):
            sT_ref[0, 2 * p] = s_scr[p, :HEAD_DIM, :HEAD_DIM]
            sT_ref[0, 2 * p + 1] = s_scr[p, HEAD_DIM:, HEAD_DIM:]


def _wkv_prompt(seqs, n_heads, arrays, ct):
    n, w = arrays[0].shape
    tiles_per_seq = n // seqs // ct
    row = pl.BlockSpec((ct, w), lambda i: (i, 0))
    st_shape = (seqs, n_heads, HEAD_DIM, HEAD_DIM)
    return pl.pallas_call(
        functools.partial(_wkv_prompt_kernel, chunks=ct // WKV_CHUNK, tiles_per_seq=tiles_per_seq),
        grid=(n // ct,),
        in_specs=[row] * 6,
        out_specs=[row, pl.BlockSpec((1,) + st_shape[1:], lambda i: (i // tiles_per_seq, 0, 0, 0))],
        out_shape=[jax.ShapeDtypeStruct((n, w), F32), jax.ShapeDtypeStruct(st_shape, F32)],
        scratch_shapes=[pltpu.VMEM((n_heads // 2, PAIR, PAIR), F32)],
        compiler_params=_cparams(1),
        name="wkv_prompt",
    )(*arrays)


def _wkv_sample_kernel(r_ref, lw_ref, k_ref, v_ref, kk_ref, kka_ref, s0_ref, y_ref, sT_ref, *, seqs_per_step):
    n_heads = s0_ref.shape[1]
    c = r_ref.shape[0]
    ri = lax.broadcasted_iota(jnp.int32, (c, c), 0)
    ci = lax.broadcasted_iota(jnp.int32, (c, c), 1)
    masks = (ri > ci, ri >= ci, (ri == ci).astype(F32))
    gens = []
    for s in range(seqs_per_step):
        r, lw, k, v, kk, kka = [ref[:, s, 0, :] for ref in (r_ref, lw_ref, k_ref, v_ref, kk_ref, kka_ref)]
        cum = _cumsum_rows(lw)
        cum_last = cum[c - 1:c, :]
        p_inv = jnp.exp(-cum)
        d_end = jnp.exp(cum_last - cum)
        ops = (kk * jnp.exp(cum - lw), r * jnp.exp(cum), k * p_inv, kka * p_inv, k * d_end, kka * d_end, v,
               jnp.exp(cum_last))
        for h in range(n_heads):
            hs = slice(h * HEAD_DIM, (h + 1) * HEAD_DIM)
            gens.append(_wkv_head_chunk(*[z[:, hs] for z in ops], s0_ref[s, h], masks))
    outs = _lockstep(gens)
    for s in range(seqs_per_step):
        heads = outs[s * n_heads:(s + 1) * n_heads]
        y_ref[:, s, 0, :] = jnp.concatenate([y for y, _ in heads], axis=1)
        for h, (_, s_new) in enumerate(heads):
            sT_ref[s, h] = s_new


def _wkv_sample(t_len, seqs, n_heads, arrays, s0, seqs_per_step):
    w = arrays[0].shape[-1]
    arrays = [a.reshape(t_len, seqs, 1, w) for a in arrays]
    tok = pl.BlockSpec((t_len, seqs_per_step, 1, w), lambda i: (0, i, 0, 0))
    st = pl.BlockSpec((seqs_per_step, n_heads, HEAD_DIM, HEAD_DIM), lambda i: (i, 0, 0, 0))
    y, s_t = pl.pallas_call(
        functools.partial(_wkv_sample_kernel, seqs_per_step=seqs_per_step),
        grid=(seqs // seqs_per_step,),
        in_specs=[tok] * 6 + [st],
        out_specs=[tok, st],
        out_shape=[jax.ShapeDtypeStruct((t_len, seqs, 1, w), F32), jax.ShapeDtypeStruct(s0.shape, F32)],
        compiler_params=_cparams(1),
        name="wkv_sample",
    )(*arrays, s0)
    return y.reshape(t_len * seqs, w), s_t


ATT_SPAN = max(DILATIONS) * ATT_BLOCK


def _attn_prompt_kernel(q_ref, kp_ref, kc_ref, vp_ref, vc_ref, o_ref, kx, vx, ob, lb, *, n_heads, tiles_per_iter):
    slab, blk = pl.program_id(1), pl.program_id(2)
    span = ATT_SPAN
    kx[:span] = kp_ref[0]
    kx[span:] = kc_ref[0]
    vx[:span] = vp_ref[0]
    vx[span:] = vc_ref[0]
    qi = lax.broadcasted_iota(jnp.int32, (ATT_BLOCK, 2 * ATT_BLOCK), 0)
    ki = lax.broadcasted_iota(jnp.int32, (ATT_BLOCK, 2 * ATT_BLOCK), 1)
    steps = ATT_BLOCK + qi - ki
    band = (steps >= 0) & (steps <= ATT_BLOCK)
    stepsf = steps.astype(F32)
    first = lax.broadcasted_iota(jnp.int32, (ATT_BLOCK, PAIR), 1) < HEAD_DIM
    scale = HEAD_DIM ** -0.5
    alibi = [jnp.exp2(jnp.zeros_like(stepsf) - (8.0 / n_heads) * (2 * slab + 1 + h2).astype(F32)) * stepsf
             for h2 in range(2)]

    def head(q_t, k_t, v_t, ok, bias, mine):
        s = lax.dot_general(jnp.where(mine, q_t, 0.0).astype(BF16), k_t, (((1,), (1,)), ((), ())),
                            preferred_element_type=F32)
        yield
        s = jnp.where(ok, s * scale - bias, -jnp.inf)
        m = jnp.max(s, axis=-1, keepdims=True)
        e = jnp.exp(s - m)
        den = jnp.sum(e, axis=-1, keepdims=True)
        o = jnp.dot(e.astype(BF16), v_t, preferred_element_type=F32)
        yield
        return o / den, m + jnp.log(den)

    def body(it, carry):
        gens, where_to = [], []
        for u in range(tiles_per_iter):
            t = it * tiles_per_iter + u
            for br, dil in enumerate(DILATIONS):
                sub = t // dil
                base = t % dil + dil * ATT_BLOCK * sub
                rows_q = pl.ds(base, ATT_BLOCK, stride=dil)
                rows_kv = pl.ds(span + base - dil * ATT_BLOCK, 2 * ATT_BLOCK, stride=dil)
                q_t = q_ref[0, rows_q, :]
                k_t = kx[rows_kv, :].astype(BF16)
                v_t = vx[rows_kv, :].astype(BF16)
                ok = band & ((blk > 0) | (sub > 0) | (ki >= ATT_BLOCK))
                for h2 in range(2):
                    gens.append(head(q_t, k_t, v_t, ok, alibi[h2] * float(dil), first if h2 == 0 else ~first))
                where_to.append((br, rows_q))
        res = _lockstep(gens)
        for i, (br, rows_q) in enumerate(where_to):
            (o0, l0), (o1, l1) = res[2 * i], res[2 * i + 1]
            ob[br, rows_q, :] = jnp.where(first, o0, o1)
            lb[br, rows_q, :] = jnp.where(first, l0, l1)
        return carry

    lax.fori_loop(0, span // ATT_BLOCK // tiles_per_iter, body, 0)
    ls = [lb[br] for br in range(len(DILATIONS))]
    m = functools.reduce(jnp.maximum, ls)
    es = [jnp.exp(l - m) for l in ls]
    o_ref[0] = sum(e * ob[br] for br, e in enumerate(es)) / sum(es)


def _attn_prompt(q, k, v, seqs, n_heads):
    n_slabs, n, _ = q.shape
    n_blk = n // seqs // ATT_SPAN
    cur = pl.BlockSpec((1, ATT_SPAN, PAIR), lambda b, s, i: (s, b * n_blk + i, 0))
    prev = pl.BlockSpec((1, ATT_SPAN, PAIR), lambda b, s, i: (s, b * n_blk + jnp.maximum(i - 1, 0), 0))
    return pl.pallas_call(
        functools.partial(_attn_prompt_kernel, n_heads=n_heads, tiles_per_iter=2),
        grid=(seqs, n_slabs, n_blk),
        in_specs=[cur, prev, cur, prev, cur],
        out_specs=cur,
        out_shape=jax.ShapeDtypeStruct(q.shape, F32),
        scratch_shapes=[pltpu.VMEM((2 * ATT_SPAN, PAIR), F32)] * 2
                       + [pltpu.VMEM((len(DILATIONS), ATT_SPAN, PAIR), F32)] * 2,
        compiler_params=_cparams(3),
        name="attn_prompt",
    )(q, k, k, v, v)


def _attn_sample_kernel(q_ref, kn_ref, vn_ref, kc_ref, vc_ref, o_ref, ko_ref, vo_ref, *, n_heads):
    t_len = q_ref.shape[0]
    w, win = kc_ref.shape[1], kc_ref.shape[2]
    rows = n_heads * t_len
    q = q_ref[:, 0, 0, :]
    kn = kn_ref[:, 0, 0, :]
    vn = vn_ref[:, 0, 0, :]
    nt = lambda a_, b_: lax.dot_general(a_, b_, (((1,), (1,)), ((), ())), preferred_element_type=F32)
    tn = lambda a_, b_: lax.dot_general(a_, b_, (((0,), (0,)), ((), ())), preferred_element_type=F32)

    pad_rows = 16
    place = (lax.broadcasted_iota(jnp.int32, (pad_rows, LANES), 1)
             == lax.broadcasted_iota(jnp.int32, (pad_rows, LANES), 0) + (LANES - t_len)).astype(BF16)
    lane = lax.broadcasted_iota(jnp.int32, (LANES, LANES), 1)

    def transposed_tail(z):
        zp = jnp.concatenate([z, jnp.zeros((pad_rows - t_len, w), F32)], axis=0)
        hi = zp.astype(BF16)
        r1 = zp - hi.astype(F32)
        mid = r1.astype(BF16)
        lo = (r1 - mid.astype(F32)).astype(BF16)
        return tn(hi, place) + tn(mid, place) + tn(lo, place)

    for src, new, dst in ((kc_ref, kn, ko_ref), (vc_ref, vn, vo_ref)):
        tail = transposed_tail(new)
        for r0 in range(0, w, LANES):
            rolled = pltpu.roll(src[0, r0:r0 + LANES, :], win - t_len, 1)
            dst[0, r0:r0 + LANES, :win - LANES] = rolled[:, :win - LANES]
            dst[0, r0:r0 + LANES, win - LANES:] = jnp.where(lane < LANES - t_len, rolled[:, win - LANES:],
                                                             tail[r0:r0 + LANES])

    qrep = jnp.concatenate([q] * n_heads, axis=0)
    rh = lax.broadcasted_iota(jnp.int32, (rows, w), 0) // t_len
    ch = lax.broadcasted_iota(jnp.int32, (rows, w), 1) // HEAD_DIM
    qexp = jnp.where(rh == ch, qrep, 0.0).astype(BF16)
    scale = HEAD_DIM ** -0.5

    def weights(n_keys, key_pos0):
        ri = lax.broadcasted_iota(jnp.int32, (rows, n_keys), 0)
        ki = lax.broadcasted_iota(jnp.int32, (rows, n_keys), 1)
        dist = win + ri % t_len - (key_pos0 + ki)
        cnt = jnp.zeros((rows, n_keys), F32)
        for wdw, dil in zip(WINDOWS, DILATIONS):
            ok = (dist >= 0) & (dist % dil == 0) & (dist <= wdw)
            cnt = cnt + jnp.where(ok, 1.0, 0.0)
        slope = jnp.exp2(-8.0 * (ri // t_len + 1).astype(F32) / n_heads)
        return cnt, slope * dist.astype(F32)

    cnt_c, bias_c = weights(win, 0)
    cnt_n, bias_n = weights(t_len, win)
    s_c = jnp.where(cnt_c > 0, jnp.dot(qexp, kc_ref[0].astype(BF16), preferred_element_type=F32) * scale - bias_c,
                    -jnp.inf)
    s_n = jnp.where(cnt_n > 0, nt(qexp, kn.astype(BF16)) * scale - bias_n, -jnp.inf)
    m = jnp.maximum(jnp.max(s_c, axis=-1, keepdims=True), jnp.max(s_n, axis=-1, keepdims=True))
    e_c = cnt_c * jnp.exp(s_c - m)
    e_n = cnt_n * jnp.exp(s_n - m)
    den = jnp.sum(e_c, axis=-1, keepdims=True) + jnp.sum(e_n, axis=-1, keepdims=True)
    acc = (nt(e_c.astype(BF16), vc_ref[0].astype(BF16)) + jnp.dot(e_n, vn, preferred_element_type=F32)) / den
    acc = jnp.where(rh == ch, acc, 0.0)
    out = acc[:t_len]
    for h in range(1, n_heads):
        out = out + acc[h * t_len:(h + 1) * t_len]
    o_ref[:, 0, 0, :] = out


def _attn_sample(q, k, v, kc, vc, t_len, seqs, n_heads):
    _, w, win = kc.shape
    tok = pl.BlockSpec((t_len, 1, 1, w), lambda b: (0, b, 0, 0))
    cache = pl.BlockSpec((1, w, win), lambda b: (b, 0, 0))
    v4 = lambda a: a.reshape(t_len, seqs, 1, w)
    o, ko, vo = pl.pallas_call(
        functools.partial(_attn_sample_kernel, n_heads=n_heads),
        grid=(seqs,),
        in_specs=[tok, tok, tok, cache, cache],
        out_specs=[tok, cache, cache],
        out_shape=[jax.ShapeDtypeStruct((t_len, seqs, 1, w), F32)] + [jax.ShapeDtypeStruct(kc.shape, F32)] * 2,
        compiler_params=_cparams(1),
        name="attn_sample",
    )(v4(q), v4(k), v4(v), kc, vc)
    return o.reshape(t_len * seqs, w), ko, vo


def _post_kernel(x_ref, y_ref, g_ref, bonus_ref, at_ref, lnw_ref, lnb_ref, gain_ref, wout_ref, nffn_ref, wr_hi_ref,
                 wr_lo_ref, br_ref, bd_ref, x1_ref, hn_ref, comb_ref, *, attn_slabs, rw):

    bd = bd_ref[...]
    y = y_ref[...]
    mean = _hilo_dot(y, bd) * (1.0 / HEAD_DIM)
    d = y - mean
    var = _hilo_dot(d * d, bd) * (1.0 / HEAD_DIM)
    yn = d * lax.rsqrt(var + GN_EPS) * lnw_ref[...] + lnb_ref[...]
    rw_out = (yn + bonus_ref[...]) * g_ref[...]

    if attn_slabs:
        at = jnp.concatenate([at_ref[s] for s in range(at_ref.shape[0])], axis=1)
    else:
        at = at_ref[...]
    at_out = _rms(at, gain_ref[...])

    x1 = (x_ref[...] + jnp.dot(rw_out.astype(BF16), wout_ref[:rw, :], preferred_element_type=F32)
          + jnp.dot(at_out.astype(BF16), wout_ref[rw:, :], preferred_element_type=F32))
    x1_ref[...] = x1
    hn = _rms(x1, nffn_ref[...])
    hn_ref[...] = hn.astype(BF16)

    hi, lo = _split2(hn)
    logits = (jnp.dot(hi, wr_hi_ref[...], preferred_element_type=F32)
              + jnp.dot(lo, wr_hi_ref[...], preferred_element_type=F32)
              + jnp.dot(hi, wr_lo_ref[...], preferred_element_type=F32)) + br_ref[...]
    lane = lax.broadcasted_iota(jnp.int32, logits.shape, 1)
    big = jnp.int32(ROUTER_LANES)
    first_max = lambda z: jnp.min(jnp.where(z == jnp.max(z, axis=-1, keepdims=True), lane, big), axis=-1, keepdims=True)
    gl = jnp.where(lane < N_GROUPS, logits, -jnp.inf)
    gsel = first_max(gl)
    gw = 1.0 / jnp.sum(jnp.exp(gl - jnp.max(gl, axis=-1, keepdims=True)), axis=-1, keepdims=True)
    e_lane = lane - EXPERT_LANE0
    in_group = (e_lane >= 0) & (e_lane < N_EXPERTS) & (e_lane // EXPERTS_PER_GROUP == gsel)
    el = jnp.where(in_group, logits, -jnp.inf)
    m1 = jnp.max(el, axis=-1, keepdims=True)
    i1 = first_max(el)
    el2 = jnp.where(lane == i1, -jnp.inf, el)
    m2 = jnp.max(el2, axis=-1, keepdims=True)
    i2 = first_max(el2)
    e2 = jnp.exp(m2 - m1)
    w1 = gw / (1.0 + e2)
    comb_ref[...] = jnp.where(lane == i1, w1, 0.0) + jnp.where(lane == i2, w1 * e2, 0.0)


def _post(x, y, g, bonus, attn, prm, rw, tm):
    n, d = x.shape
    row = lambda w: pl.BlockSpec((tm, w), lambda i: (i, 0))
    attn_slabs = attn.ndim == 3
    at_spec = pl.BlockSpec((attn.shape[0], tm, PAIR), lambda i: (0, i, 0)) if attn_slabs else row(attn.shape[1])
    params = [prm[k] for k in ("ln_x_w", "ln_x_b", "attn_gain", "w_out", "norm_ffn", "wr_hi", "wr_lo", "b_router", "bd")]
    return pl.pallas_call(
        functools.partial(_post_kernel, attn_slabs=attn_slabs, rw=rw),
        grid=(n // tm,),
        in_specs=[row(d), row(rw), row(rw), row(rw), at_spec] + [_full(p.shape) for p in params],
        out_specs=[row(d), row(d), row(ROUTER_LANES)],
        out_shape=[jax.ShapeDtypeStruct((n, d), F32), jax.ShapeDtypeStruct((n, d), BF16),
                   jax.ShapeDtypeStruct((n, ROUTER_LANES), F32)],
        compiler_params=_cparams(1),
        name="post",
    )(x, y, g, bonus, attn, *params)


def _moe_kernel(hn_ref, comb_ref, wg_ref, wu_ref, wd_ref, o_ref):
    e = pl.program_id(1)

    @pl.when(e == 0)
    def _():
        o_ref[...] = jnp.zeros_like(o_ref)

    hn = hn_ref[...]
    gate = jnp.dot(hn, wg_ref[0], preferred_element_type=F32)
    up = jnp.dot(hn, wu_ref[0], preferred_element_type=F32)
    act = (gate * jax.nn.sigmoid(gate) * up).astype(BF16)
    out = jnp.dot(act, wd_ref[0], preferred_element_type=F32)
    comb = comb_ref[...]
    lane = lax.broadcasted_iota(jnp.int32, comb.shape, 1)
    c = jnp.sum(jnp.where(lane == e + EXPERT_LANE0, comb, 0.0), axis=-1, keepdims=True)
    o_ref[...] += c * out


def _moe(hn, comb, wg, wu, wd, tm):
    n, d = hn.shape
    n_exp, _, de = wg.shape
    return pl.pallas_call(
        _moe_kernel,
        grid=(n // tm, n_exp),
        in_specs=[pl.BlockSpec((tm, d), lambda i, e: (i, 0)),
                  pl.BlockSpec((tm, ROUTER_LANES), lambda i, e: (i, 0)),
                  pl.BlockSpec((1, d, de), lambda i, e: (e, 0, 0)),
                  pl.BlockSpec((1, d, de), lambda i, e: (e, 0, 0)),
                  pl.BlockSpec((1, de, d), lambda i, e: (e, 0, 0))],
        out_specs=pl.BlockSpec((tm, d), lambda i, e: (i, 0)),
        out_shape=jax.ShapeDtypeStruct((n, d), F32),
        compiler_params=_cparams(2),
        name="moe",
    )(hn, comb, wg, wu, wd)


def _tail_kernel(x1_ref, moe_ref, p_ref, nple_ref, wgate_ref, wple_ref, nfin_ref, y_ref):
    x2 = x1_ref[...] + moe_ref[...]
    gate = jax.nn.sigmoid(jnp.dot(_rms(x2, nple_ref[...]).astype(BF16), wgate_ref[...], preferred_element_type=F32))
    x3 = x2 + jnp.dot(p_ref[...].astype(BF16), wple_ref[...], preferred_element_type=F32) * gate
    y_ref[...] = _rms(x3, nfin_ref[...])


def _tail(x1, moe, p, prm, tm):
    n, d = x1.shape
    row = lambda w: pl.BlockSpec((tm, w), lambda i: (i, 0))
    params = [prm[k] for k in ("norm_ple", "w_ple_gate", "w_ple", "norm_final")]
    return pl.pallas_call(
        _tail_kernel,
        grid=(n // tm,),
        in_specs=[row(d), row(d), row(p.shape[1])] + [_full(a.shape) for a in params],
        out_specs=row(d),
        out_shape=jax.ShapeDtypeStruct((n, d), F32),
        compiler_params=_cparams(1),
        name="tail",
    )(x1, moe, p, *params)


def _layer_params(i, norm_mix, w_in, mu, w0, w2, a0, a2, g2, k_k, k_a, r_k, ln_x_w, ln_x_b, attn_gain, w_out,
                  norm_ffn, w_group, b_group, w_expert_router, b_expert_router, w_gate, w_up, w_down,
                  norm_ple, w_ple, w_ple_gate, norm_final):
    rw = w0.shape[1]
    d = w_in.shape[1]
    vec = lambda a: a[i].reshape(1, -1)
    lora_rows = DECAY_LORA + AAA_LORA + GATE_LORA

    def lora(w, lo):
        return jnp.zeros((lora_rows, rw), F32).at[lo:lo + w.shape[0]].set(w).astype(BF16)

    head = np.arange(rw) // HEAD_DIM
    w_router = jnp.zeros((d, ROUTER_LANES), F32)
    w_router = w_router.at[:, :N_GROUPS].set(w_group[i]).at[:, EXPERT_LANE0:EXPERT_LANE0 + N_EXPERTS].set(w_expert_router[i])
    wr_hi, wr_lo = _split2(w_router)
    b_router = jnp.zeros((1, ROUTER_LANES), F32)
    b_router = b_router.at[0, :N_GROUPS].set(b_group[i]).at[0, EXPERT_LANE0:EXPERT_LANE0 + N_EXPERTS].set(b_expert_router[i])
    return dict(
        norm_mix=vec(norm_mix), w_in=w_in[i].astype(BF16), mu=vec(mu), w0=vec(w0), a0=vec(a0),
        w2p=lora(w2[i], 0), a2p=lora(a2[i], DECAY_LORA), g2p=lora(g2[i], DECAY_LORA + AAA_LORA),
        k_k=vec(k_k), k_a=vec(k_a), r_k=vec(r_k), ln_x_w=vec(ln_x_w), ln_x_b=vec(ln_x_b),
        bd=jnp.asarray(head[:, None] == head[None, :], BF16),
        attn_gain=vec(attn_gain), w_out=w_out[i].astype(BF16), norm_ffn=vec(norm_ffn),
        wr_hi=wr_hi, wr_lo=wr_lo, b_router=b_router,
        w_gate=w_gate[i].astype(BF16), w_up=w_up[i].astype(BF16), w_down=w_down[i].astype(BF16),
        norm_ple=vec(norm_ple), w_ple=w_ple[i].astype(BF16), w_ple_gate=w_ple_gate[i].astype(BF16),
        norm_final=norm_final.reshape(1, -1),
    )


def _tile(n, want):
    t = min(n, want)
    assert n % t == 0, (n, t)
    return t


def _ffn_and_tail(x, y, g, bonus, attn, p, prm, rw):
    n = x.shape[0]
    x1, hn, comb = _post(x, y, g, bonus, attn, prm, rw, _tile(n, 256))
    moe = _moe(hn, comb, prm["w_gate"], prm["w_up"], prm["w_down"], _tile(n, 1024))
    return _tail(x1, moe, p, prm, _tile(n, 256))


def kernel(x_prompt, x_sample, state_wkv, state_shift, cache_k_win, cache_v_win, p_prompt, p_sample, norm_mix, w_in, mu, w0, w2, a0, a2, g2, k_k, k_a, r_k, ln_x_w, ln_x_b, attn_gain, w_out, norm_ffn, w_group, b_group, w_expert_router, b_expert_router, w_gate, w_up, w_down, norm_ple, w_ple, w_ple_gate, norm_final):
    depth = norm_mix.shape[0]
    assert depth == 1, "single-layer step"
    bp, s_len, d = x_prompt.shape
    bs, t_len, _ = x_sample.shape
    rw = w0.shape[1]
    n_rw_heads = rw // HEAD_DIM
    rw_cols = mu.shape[1]
    at_width = (w_in.shape[2] - rw_cols) // 3
    n_at_heads = at_width // HEAD_DIM
    win = cache_k_win.shape[2]
    assert s_len % (max(DILATIONS) * ATT_BLOCK) == 0 and s_len >= win and t_len == 8 and bs % 8 == 0
    assert all(wd // dl == ATT_BLOCK for wd, dl in zip(WINDOWS, DILATIONS)) and win >= max(WINDOWS)

    prm = _layer_params(0, norm_mix, w_in, mu, w0, w2, a0, a2, g2, k_k, k_a, r_k, ln_x_w, ln_x_b, attn_gain, w_out,
                        norm_ffn, w_group, b_group, w_expert_router, b_expert_router, w_gate, w_up, w_down,
                        norm_ple, w_ple, w_ple_gate, norm_final)

    n_p = bp * s_len
    xp = x_prompt.reshape(n_p, d)
    pr, q, k, v = _in_proj(xp, prm["norm_mix"], prm["w_in"], rw_cols, at_width, _tile(n_p, 256), True)
    tm = _tile(s_len, 512)
    r_, lw, km, vv, kk, kka, g, bonus = _rwkv_prep(pr, pr, prm, rw, tm, 1, s_len // tm)
    ct = _tile(s_len, 256)
    y, wkv_p = _wkv_prompt(bp, n_rw_heads, (r_, lw, km, vv, kk, kka), ct)
    attn = _attn_prompt(q, k, v, bp, n_at_heads)
    y_prompt = _ffn_and_tail(xp, y, g, bonus, attn, p_prompt[0].reshape(n_p, -1), prm, rw).reshape(bp, s_len, d)
    shift_p = pr.reshape(bp, s_len, rw_cols)[:, -1]
    keep = min(win, s_len)

    def window(slabs):
        z = slabs.reshape(-1, bp, s_len, 2, HEAD_DIM)[:, :, s_len - keep:]
        return jnp.transpose(z, (1, 2, 0, 3, 4)).reshape(bp, keep, n_at_heads, HEAD_DIM)

    kwin_p, vwin_p = window(k), window(v)

    n_s = bs * t_len
    xs = jnp.swapaxes(x_sample, 0, 1).reshape(n_s, d)
    ps = jnp.swapaxes(p_sample[0], 0, 1).reshape(n_s, -1)
    pr_s, q_s, k_s, v_s = _in_proj(xs, prm["norm_mix"], prm["w_in"], rw_cols, at_width, _tile(n_s, 256), False)
    r_, lw, km, vv, kk, kka, g, bonus = _rwkv_prep(pr_s, state_shift[0], prm, rw, n_s, bs, 1)
    y, wkv_s = _wkv_sample(t_len, bs, n_rw_heads, (r_, lw, km, vv, kk, kka), state_wkv[0], 4)
    to_t = lambda c: jnp.transpose(c[0], (0, 2, 3, 1)).reshape(bs, at_width, win)
    from_t = lambda c: jnp.transpose(c.reshape(bs, n_at_heads, HEAD_DIM, win), (0, 3, 1, 2))[None]
    at_s, kwin_s, vwin_s = _attn_sample(q_s, k_s, v_s, to_t(cache_k_win), to_t(cache_v_win), t_len, bs, n_at_heads)
    y_s = _ffn_and_tail(xs, y, g, bonus, at_s, ps, prm, rw)
    y_sample = jnp.swapaxes(y_s.reshape(t_len, bs, d), 0, 1)
    shift_s = pr_s[(t_len - 1) * bs:]

    return (y_prompt, y_sample, wkv_p[None], shift_p[None], kwin_p[None], vwin_p[None],
            wkv_s[None], shift_s[None], from_t(kwin_s), from_t(vwin_s))
```

```python
import functools

import numpy as np
import jax
import jax.numpy as jnp
from jax import lax
from jax.experimental import pallas as pl
from jax.experimental.pallas import tpu as pltpu

F32 = jnp.float32
BF16 = jnp.bfloat16

LANES = 128
HEAD_DIM = 64
PAIR = 2 * HEAD_DIM
DECAY_LORA = 32
AAA_LORA = 32
GATE_LORA = 64
GN_EPS = 64e-5
RMS_EPS = 1e-6
WINDOWS = (128, 512, 2048)
DILATIONS = (1, 4, 16)
ATT_BLOCK = 128
N_GROUPS = 4
EXPERTS_PER_GROUP = 8
N_EXPERTS = N_GROUPS * EXPERTS_PER_GROUP
ROUTER_LANES = 128
EXPERT_LANE0 = N_GROUPS
WKV_CHUNK = 64
VMEM_LIMIT = 56 * 1024 * 1024


def _cparams(n_axes):
    return pltpu.CompilerParams(dimension_semantics=("arbitrary",) * n_axes, vmem_limit_bytes=VMEM_LIMIT)


def _full(shape):
    nd = len(shape)
    return pl.BlockSpec(shape, lambda *_: (0,) * nd)


def _bdot(a, b):
    return jnp.dot(a.astype(BF16), b.astype(BF16), preferred_element_type=F32)


def _split2(a):
    hi = a.astype(BF16)
    lo = (a - hi.astype(F32)).astype(BF16)
    return hi, lo


def _hilo_dot(a, b_bf16):
    hi, lo = _split2(a)
    return (jnp.dot(hi, b_bf16, preferred_element_type=F32)
            + jnp.dot(lo, b_bf16, preferred_element_type=F32))


def _rms(x, g):
    return x * lax.rsqrt(jnp.mean(x * x, axis=-1, keepdims=True) + RMS_EPS) * g


def _in_proj_kernel(x_ref, g_ref, w_ref, pr_ref, q_ref, k_ref, v_ref, *, rw_cols, at_width, slabs):
    h = _rms(x_ref[...], g_ref[...]).astype(BF16)
    pr_ref[...] = jnp.dot(h, w_ref[:, :rw_cols], preferred_element_type=F32)
    for i, o_ref in enumerate((q_ref, k_ref, v_ref)):
        lo = rw_cols + i * at_width
        z = jnp.dot(h, w_ref[:, lo:lo + at_width], preferred_element_type=F32)
        if slabs:
            for s in range(at_width // PAIR):
                o_ref[s] = z[:, s * PAIR:(s + 1) * PAIR]
        else:
            o_ref[...] = z


def _in_proj(x, g, w_bf16, rw_cols, at_width, tm, slabs):
    n, d = x.shape
    row = lambda w: pl.BlockSpec((tm, w), lambda i: (i, 0))
    if slabs:
        qkv_spec = pl.BlockSpec((at_width // PAIR, tm, PAIR), lambda i: (0, i, 0))
        qkv_shape = jax.ShapeDtypeStruct((at_width // PAIR, n, PAIR), F32)
    else:
        qkv_spec, qkv_shape = row(at_width), jax.ShapeDtypeStruct((n, at_width), F32)
    return pl.pallas_call(
        functools.partial(_in_proj_kernel, rw_cols=rw_cols, at_width=at_width, slabs=slabs),
        grid=(n // tm,),
        in_specs=[row(d), _full(g.shape), _full(w_bf16.shape)],
        out_specs=[row(rw_cols)] + [qkv_spec] * 3,
        out_shape=[jax.ShapeDtypeStruct((n, rw_cols), F32)] + [qkv_shape] * 3,
        compiler_params=_cparams(1),
        name="in_proj",
    )(x, g, w_bf16)


def _rwkv_prep_kernel(pr_ref, bnd_ref, mu_ref, w0_ref, w2_ref, a0_ref, a2_ref, g2_ref, kk_ref, ka_ref, rk_ref,
                      bd_ref, r_o, lw_o, k_o, v_o, kk_o, kka_o, g_o, bonus_o, *, rw, shift_rows, tiles_per_seq):
    pr = pr_ref[...]
    tm = pr.shape[0]
    if shift_rows == 1:
        first = pl.program_id(0) % tiles_per_seq == 0
        prev_row = jnp.where(first, 0.0, bnd_ref[7:8, :])
        row = lax.broadcasted_iota(jnp.int32, pr.shape, 0)
        prev = jnp.where(row == 0, prev_row, pltpu.roll(pr, 1, 0))
    else:
        prev = jnp.concatenate([bnd_ref[...], pr[:tm - shift_rows]], axis=0)
    xm = pr + (prev - pr) * mu_ref[...]
    xr, xk, xv, xl = xm[:, :rw], xm[:, rw:2 * rw], xm[:, 2 * rw:3 * rw], xm[:, 3 * rw:]
    bd = bd_ref[...]
    z = -(w0_ref[...] + _bdot(jnp.tanh(xl), w2_ref[...]))
    softplus = jnp.maximum(z, 0.0) + jnp.log(1.0 + jnp.exp(-jnp.abs(z)))
    lw = -jnp.exp(-softplus - 0.5)
    a = jax.nn.sigmoid(a0_ref[...] + _bdot(xl, a2_ref[...]))
    g = _bdot(jax.nn.sigmoid(xl), g2_ref[...])
    kk = xk * kk_ref[...]
    kk = kk / jnp.maximum(jnp.sqrt(_hilo_dot(kk * kk, bd)), 1e-12)
    kmod = xk * (1.0 + (a - 1.0) * ka_ref[...])
    bonus = _hilo_dot(xr * kmod * rk_ref[...], bd) * xv
    r_o[...] = xr
    lw_o[...] = lw
    k_o[...] = kmod
    v_o[...] = xv
    kk_o[...] = kk
    kka_o[...] = kk * a
    g_o[...] = g
    bonus_o[...] = bonus


def _rwkv_prep(pr, bnd, prm, rw, tm, shift_rows, tiles_per_seq):
    n, c = pr.shape
    row = lambda w: pl.BlockSpec((tm, w), lambda i: (i, 0))
    if shift_rows == 1:
        bnd_spec = pl.BlockSpec((8, c), lambda i: (jnp.maximum(i * (tm // 8) - 1, 0), 0))
    else:
        bnd_spec = _full(bnd.shape)
    params = [prm[k] for k in ("mu", "w0", "w2p", "a0", "a2p", "g2p", "k_k", "k_a", "r_k", "bd")]
    return pl.pallas_call(
        functools.partial(_rwkv_prep_kernel, rw=rw, shift_rows=shift_rows, tiles_per_seq=tiles_per_seq),
        grid=(n // tm,),
        in_specs=[row(c), bnd_spec] + [_full(p.shape) for p in params],
        out_specs=[row(rw)] * 8,
        out_shape=[jax.ShapeDtypeStruct((n, rw), F32)] * 8,
        compiler_params=_cparams(1),
        name="rwkv_prep",
    )(pr, bnd, *params)


def _cumsum_rows(x):
    n = x.shape[0]
    row = lax.broadcasted_iota(jnp.int32, x.shape, 0)
    s = 1
    while s < n:
        x = x + jnp.where(row >= s, pltpu.roll(x, s, 0), 0.0)
        s *= 2
    return x


def _lockstep(gens):
    results = [None] * len(gens)
    live = list(range(len(gens)))
    while live:
        still = []
        for i in live:
            try:
                next(gens[i])
                still.append(i)
            except StopIteration as stop:
                results[i] = stop.value
        live = still
    return results


def _wkv_head_chunk(pt, rt, kt, qt, kt_end, qt_end, vh, p_end, s_h, masks):
    strict, incl, eye = masks
    c = pt.shape[0]
    nt = lambda a_, b_: lax.dot_general(a_, b_, (((1,), (1,)), ((), ())), preferred_element_type=F32)
    nn = lambda a_, b_: jnp.dot(a_, b_, preferred_element_type=F32)
    tn = lambda a_, b_: lax.dot_general(a_, b_, (((0,), (0,)), ((), ())), preferred_element_type=F32)
    lhs = jnp.concatenate([pt, rt], axis=0)
    gq = nt(lhs, qt)
    gk = nt(lhs, kt)
    yield
    l_qp = jnp.where(strict, gq[:c], 0.0)
    l_kp = jnp.where(strict, gk[:c], 0.0)
    a_rq = jnp.where(incl, gq[c:], 0.0)
    a_rk = jnp.where(incl, gk[c:], 0.0)
    t_inv = eye - l_qp
    pw = -l_qp
    lv_av = nn(jnp.concatenate([l_kp, a_rk], axis=0), vh)
    for _ in range(int(np.log2(c)) - 1):
        pw = nn(pw, pw)
        yield
        t_new = nn(pw, t_inv)
        yield
        t_inv = t_inv + t_new
    w_mat = nn(t_inv, pt)
    u_v = nn(t_inv, lv_av[:c])
    yield
    ws = nt(jnp.concatenate([w_mat, rt], axis=0), s_h)
    yield
    u = u_v + ws[:c]
    au = nn(a_rq, u)
    upd = tn(jnp.concatenate([vh, -u], axis=0), jnp.concatenate([kt_end, qt_end], axis=0))
    yield
    return ws[c:] + lv_av[c:] - au, s_h * p_end + upd


def _pair_stack(z, first):
    return jnp.concatenate([jnp.where(first, z, 0.0), jnp.where(first, 0.0, z)], axis=0).astype(BF16)


def _wkv_pair_static(slabs, masks):
    r, lw, k, v, kk, kka = slabs
    first, strict, incl, eye = masks
    c = r.shape[0]
    n2 = 2 * c
    nt = lambda a_, b_: lax.dot_general(a_, b_, (((1,), (1,)), ((), ())), preferred_element_type=F32)
    nn = lambda a_, b_: jnp.dot(a_, b_, preferred_element_type=F32)
    cum = _cumsum_rows(lw)
    cum_last = cum[c - 1:c, :]
    d_end = jnp.exp(cum_last - cum)
    p_inv = jnp.exp(-cum)
    ps = _pair_stack(kk * jnp.exp(cum - lw), first)
    rs = _pair_stack(r * jnp.exp(cum), first)
    qs = _pair_stack(kka * p_inv, first)
    ks = _pair_stack(k * p_inv, first)
    vs = _pair_stack(v, first)
    ends = jnp.concatenate([_pair_stack(k * d_end, first), _pair_stack(kka * d_end, first)], axis=0)
    g = nt(jnp.concatenate([ps, rs], axis=0), jnp.concatenate([qs, ks], axis=0))
    yield
    l_qp = jnp.where(strict, g[:n2, :n2], 0.0)
    l_kp = jnp.where(strict, g[:n2, n2:], 0.0).astype(BF16)
    a_rq = jnp.where(incl, g[n2:, :n2], 0.0).astype(BF16)
    a_rk = jnp.where(incl, g[n2:, n2:], 0.0).astype(BF16)
    t_inv = eye - l_qp
    pw = (-l_qp).astype(BF16)
    lv_av = nn(jnp.concatenate([l_kp, a_rk], axis=0), vs)
    pw2 = nn(pw, pw)
    yield
    lv, av = lv_av[:n2].astype(BF16), lv_av[n2:]
    pw = pw2.astype(BF16)
    n_it = int(np.log2(c)) - 1
    for i in range(n_it):
        if i < n_it - 1:
            out = nn(pw, jnp.concatenate([pw, t_inv.astype(BF16)], axis=1))
            yield
            pw = out[:, :n2].astype(BF16)
            t_inv = t_inv + out[:, n2:]
        else:
            out = nn(pw, t_inv.astype(BF16))
            yield
            t_inv = t_inv + out
    tp = nn(t_inv.astype(BF16), jnp.concatenate([ps, lv], axis=1))
    yield
    return dict(w=tp[:, :PAIR].astype(BF16), u_v=tp[:, PAIR:], rs=rs, vs=vs, a_rq=a_rq, av=av, ends=ends,
                p_end=jnp.exp(cum_last))


def _wkv_pair_step(st, s_bd):
    nt = lambda a_, b_: lax.dot_general(a_, b_, (((1,), (1,)), ((), ())), preferred_element_type=F32)
    tn = lambda a_, b_: lax.dot_general(a_, b_, (((0,), (0,)), ((), ())), preferred_element_type=F32)
    n2 = st["w"].shape[0]
    ws = nt(jnp.concatenate([st["w"], st["rs"]], axis=0), s_bd.astype(BF16))
    yield
    u = st["u_v"] + ws[:n2]
    upd = tn(jnp.concatenate([st["vs"], (-u).astype(BF16)], axis=0), st["ends"])
    au = jnp.dot(st["a_rq"], u.astype(BF16), preferred_element_type=F32)
    yield
    y_bd = ws[n2:] + st["av"] - au
    return y_bd[:n2 // 2] + y_bd[n2 // 2:], s_bd * st["p_end"] + upd


def _wkv_prompt_kernel(r_ref, lw_ref, k_ref, v_ref, kk_ref, kka_ref, y_ref, sT_ref, s_scr, *, chunks, tiles_per_seq):
    j = pl.program_id(0) % tiles_per_seq
    n_pairs = s_scr.shape[0]
    c = WKV_CHUNK

    @pl.when(j == 0)
    def _():
        s_scr[...] = jnp.zeros_like(s_scr)

    lane = lax.broadcasted_iota(jnp.int32, (c, PAIR), 1)
    ri = lax.broadcasted_iota(jnp.int32, (2 * c, 2 * c), 0)
    ci = lax.broadcasted_iota(jnp.int32, (2 * c, 2 * c), 1)
    rt_, ct_ = ri % c, ci % c
    masks = (lane < HEAD_DIM, rt_ > ct_, rt_ >= ct_, (ri == ci).astype(F32))

    gens = []
    for ch in range(chunks):
        rows = slice(ch * c, (ch + 1) * c)
        for p in range(n_pairs):
            lanes = slice(p * PAIR, (p + 1) * PAIR)
            slabs = [ref[rows, lanes] for ref in (r_ref, lw_ref, k_ref, v_ref, kk_ref, kka_ref)]
            gens.append(_wkv_pair_static(slabs, masks))
    static = _lockstep(gens)
    states = [s_scr[p] for p in range(n_pairs)]
    for ch in range(chunks):
        outs = _lockstep([_wkv_pair_step(static[ch * n_pairs + p], states[p]) for p in range(n_pairs)])
        for p, (y, s_new) in enumerate(outs):
            states[p] = s_new
            y_ref[ch * c:(ch + 1) * c, p * PAIR:(p + 1) * PAIR] = y
    for p in range(n_pairs):
        s_scr[p] = states[p]

    @pl.when(j == tiles_per_seq - 1)
    def _():
        for p in range(n_pairs):
            sT_ref[0, 2 * p] = s_scr[p, :HEAD_DIM, :HEAD_DIM]
            sT_ref[0, 2 * p + 1] = s_scr[p, HEAD_DIM:, HEAD_DIM:]


def _wkv_prompt(seqs, n_heads, arrays, ct):
    n, w = arrays[0].shape
    tiles_per_seq = n // seqs // ct
    row = pl.BlockSpec((ct, w), lambda i: (i, 0))
    st_shape = (seqs, n_heads, HEAD_DIM, HEAD_DIM)
    return pl.pallas_call(
        functools.partial(_wkv_prompt_kernel, chunks=ct // WKV_CHUNK, tiles_per_seq=tiles_per_seq),
        grid=(n // ct,),
        in_specs=[row] * 6,
        out_specs=[row, pl.BlockSpec((1,) + st_shape[1:], lambda i: (i // tiles_per_seq, 0, 0, 0))],
        out_shape=[jax.ShapeDtypeStruct((n, w), F32), jax.ShapeDtypeStruct(st_shape, F32)],
        scratch_shapes=[pltpu.VMEM((n_heads // 2, PAIR, PAIR), F32)],
        compiler_params=_cparams(1),
        name="wkv_prompt",
    )(*arrays)


def _wkv_sample_kernel(r_ref, lw_ref, k_ref, v_ref, kk_ref, kka_ref, s0_ref, y_ref, sT_ref, *, seqs_per_step):
    n_heads = s0_ref.shape[1]
    c = r_ref.shape[0]
    ri = lax.broadcasted_iota(jnp.int32, (c, c), 0)
    ci = lax.broadcasted_iota(jnp.int32, (c, c), 1)
    masks = (ri > ci, ri >= ci, (ri == ci).astype(F32))
    gens = []
    for s in range(seqs_per_step):
        r, lw, k, v, kk, kka = [ref[:, s, 0, :] for ref in (r_ref, lw_ref, k_ref, v_ref, kk_ref, kka_ref)]
        cum = _cumsum_rows(lw)
        cum_last = cum[c - 1:c, :]
        p_inv = jnp.exp(-cum)
        d_end = jnp.exp(cum_last - cum)
        ops = (kk * jnp.exp(cum - lw), r * jnp.exp(cum), k * p_inv, kka * p_inv, k * d_end, kka * d_end, v,
               jnp.exp(cum_last))
        for h in range(n_heads):
            hs = slice(h * HEAD_DIM, (h + 1) * HEAD_DIM)
            gens.append(_wkv_head_chunk(*[z[:, hs] for z in ops], s0_ref[s, h], masks))
    outs = _lockstep(gens)
    for s in range(seqs_per_step):
        heads = outs[s * n_heads:(s + 1) * n_heads]
        y_ref[:, s, 0, :] = jnp.concatenate([y for y, _ in heads], axis=1)
        for h, (_, s_new) in enumerate(heads):
            sT_ref[s, h] = s_new


def _wkv_sample(t_len, seqs, n_heads, arrays, s0, seqs_per_step):
    w = arrays[0].shape[-1]
    arrays = [a.reshape(t_len, seqs, 1, w) for a in arrays]
    tok = pl.BlockSpec((t_len, seqs_per_step, 1, w), lambda i: (0, i, 0, 0))
    st = pl.BlockSpec((seqs_per_step, n_heads, HEAD_DIM, HEAD_DIM), lambda i: (i, 0, 0, 0))
    y, s_t = pl.pallas_call(
        functools.partial(_wkv_sample_kernel, seqs_per_step=seqs_per_step),
        grid=(seqs // seqs_per_step,),
        in_specs=[tok] * 6 + [st],
        out_specs=[tok, st],
        out_shape=[jax.ShapeDtypeStruct((t_len, seqs, 1, w), F32), jax.ShapeDtypeStruct(s0.shape, F32)],
        compiler_params=_cparams(1),
        name="wkv_sample",
    )(*arrays, s0)
    return y.reshape(t_len * seqs, w), s_t


ATT_SPAN = max(DILATIONS) * ATT_BLOCK


def _attn_prompt_kernel(q_ref, kp_ref, kc_ref, vp_ref, vc_ref, o_ref, kx, vx, ob, lb, *, n_heads, tiles_per_iter):
    slab, blk = pl.program_id(1), pl.program_id(2)
    span = ATT_SPAN
    kx[:span] = kp_ref[0]
    kx[span:] = kc_ref[0]
    vx[:span] = vp_ref[0]
    vx[span:] = vc_ref[0]
    qi = lax.broadcasted_iota(jnp.int32, (ATT_BLOCK, 2 * ATT_BLOCK), 0)
    ki = lax.broadcasted_iota(jnp.int32, (ATT_BLOCK, 2 * ATT_BLOCK), 1)
    steps = ATT_BLOCK + qi - ki
    band = (steps >= 0) & (steps <= ATT_BLOCK)
    stepsf = steps.astype(F32)
    first = lax.broadcasted_iota(jnp.int32, (ATT_BLOCK, PAIR), 1) < HEAD_DIM
    scale = HEAD_DIM ** -0.5
    alibi = [jnp.exp2(jnp.zeros_like(stepsf) - (8.0 / n_heads) * (2 * slab + 1 + h2).astype(F32)) * stepsf
             for h2 in range(2)]

    def head(q_t, k_t, v_t, ok, bias, mine):
        s = lax.dot_general(jnp.where(mine, q_t, 0.0).astype(BF16), k_t, (((1,), (1,)), ((), ())),
                            preferred_element_type=F32)
        yield
        s = jnp.where(ok, s * scale - bias, -jnp.inf)
        m = jnp.max(s, axis=-1, keepdims=True)
        e = jnp.exp(s - m)
        den = jnp.sum(e, axis=-1, keepdims=True)
        o = jnp.dot(e.astype(BF16), v_t, preferred_element_type=F32)
        yield
        return o / den, m + jnp.log(den)

    def body(it, carry):
        gens, where_to = [], []
        for u in range(tiles_per_iter):
            t = it * tiles_per_iter + u
            for br, dil in enumerate(DILATIONS):
                sub = t // dil
                base = t % dil + dil * ATT_BLOCK * sub
                rows_q = pl.ds(base, ATT_BLOCK, stride=dil)
                rows_kv = pl.ds(span + base - dil * ATT_BLOCK, 2 * ATT_BLOCK, stride=dil)
                q_t = q_ref[0, rows_q, :]
                k_t = kx[rows_kv, :].astype(BF16)
                v_t = vx[rows_kv, :].astype(BF16)
                ok = band & ((blk > 0) | (sub > 0) | (ki >= ATT_BLOCK))
                for h2 in range(2):
                    gens.append(head(q_t, k_t, v_t, ok, alibi[h2] * float(dil), first if h2 == 0 else ~first))
                where_to.append((br, rows_q))
        res = _lockstep(gens)
        for i, (br, rows_q) in enumerate(where_to):
            (o0, l0), (o1, l1) = res[2 * i], res[2 * i + 1]
            ob[br, rows_q, :] = jnp.where(first, o0, o1)
            lb[br, rows_q, :] = jnp.where(first, l0, l1)
        return carry

    lax.fori_loop(0, span // ATT_BLOCK // tiles_per_iter, body, 0)
    ls = [lb[br] for br in range(len(DILATIONS))]
    m = functools.reduce(jnp.maximum, ls)
    es = [jnp.exp(l - m) for l in ls]
    o_ref[0] = sum(e * ob[br] for br, e in enumerate(es)) / sum(es)


def _attn_prompt(q, k, v, seqs, n_heads):
    n_slabs, n, _ = q.shape
    n_blk = n // seqs // ATT_SPAN
    cur = pl.BlockSpec((1, ATT_SPAN, PAIR), lambda b, s, i: (s, b * n_blk + i, 0))
    prev = pl.BlockSpec((1, ATT_SPAN, PAIR), lambda b, s, i: (s, b * n_blk + jnp.maximum(i - 1, 0), 0))
    return pl.pallas_call(
        functools.partial(_attn_prompt_kernel, n_heads=n_heads, tiles_per_iter=2),
        grid=(seqs, n_slabs, n_blk),
        in_specs=[cur, prev, cur, prev, cur],
        out_specs=cur,
        out_shape=jax.ShapeDtypeStruct(q.shape, F32),
        scratch_shapes=[pltpu.VMEM((2 * ATT_SPAN, PAIR), F32)] * 2
                       + [pltpu.VMEM((len(DILATIONS), ATT_SPAN, PAIR), F32)] * 2,
        compiler_params=_cparams(3),
        name="attn_prompt",
    )(q, k, k, v, v)


def _attn_sample_kernel(q_ref, kn_ref, vn_ref, kc_ref, vc_ref, o_ref, ko_ref, vo_ref, *, n_heads):
    t_len = q_ref.shape[0]
    w, win = kc_ref.shape[1], kc_ref.shape[2]
    rows = n_heads * t_len
    q = q_ref[:, 0, 0, :]
    kn = kn_ref[:, 0, 0, :]
    vn = vn_ref[:, 0, 0, :]
    nt = lambda a_, b_: lax.dot_general(a_, b_, (((1,), (1,)), ((), ())), preferred_element_type=F32)
    tn = lambda a_, b_: lax.dot_general(a_, b_, (((0,), (0,)), ((), ())), preferred_element_type=F32)

    pad_rows = 16
    place = (lax.broadcasted_iota(jnp.int32, (pad_rows, LANES), 1)
             == lax.broadcasted_iota(jnp.int32, (pad_rows, LANES), 0) + (LANES - t_len)).astype(BF16)
    lane = lax.broadcasted_iota(jnp.int32, (LANES, LANES), 1)

    def transposed_tail(z):
        zp = jnp.concatenate([z, jnp.zeros((pad_rows - t_len, w), F32)], axis=0)
        hi = zp.astype(BF16)
        r1 = zp - hi.astype(F32)
        mid = r1.astype(BF16)
        lo = (r1 - mid.astype(F32)).astype(BF16)
        return tn(hi, place) + tn(mid, place) + tn(lo, place)

    for src, new, dst in ((kc_ref, kn, ko_ref), (vc_ref, vn, vo_ref)):
        tail = transposed_tail(new)
        for r0 in range(0, w, LANES):
            rolled = pltpu.roll(src[0, r0:r0 + LANES, :], win - t_len, 1)
            dst[0, r0:r0 + LANES, :win - LANES] = rolled[:, :win - LANES]
            dst[0, r0:r0 + LANES, win - LANES:] = jnp.where(lane < LANES - t_len, rolled[:, win - LANES:],
                                                             tail[r0:r0 + LANES])

    qrep = jnp.concatenate([q] * n_heads, axis=0)
    rh = lax.broadcasted_iota(jnp.int32, (rows, w), 0) // t_len
    ch = lax.broadcasted_iota(jnp.int32, (rows, w), 1) // HEAD_DIM
    qexp = jnp.where(rh == ch, qrep, 0.0).astype(BF16)
    scale = HEAD_DIM ** -0.5

    def weights(n_keys, key_pos0):
        ri = lax.broadcasted_iota(jnp.int32, (rows, n_keys), 0)
        ki = lax.broadcasted_iota(jnp.int32, (rows, n_keys), 1)
        dist = win + ri % t_len - (key_pos0 + ki)
        cnt = jnp.zeros((rows, n_keys), F32)
        for wdw, dil in zip(WINDOWS, DILATIONS):
            ok = (dist >= 0) & (dist % dil == 0) & (dist <= wdw)
            cnt = cnt + jnp.where(ok, 1.0, 0.0)
        slope = jnp.exp2(-8.0 * (ri // t_len + 1).astype(F32) / n_heads)
        return cnt, slope * dist.astype(F32)

    cnt_c, bias_c = weights(win, 0)
    cnt_n, bias_n = weights(t_len, win)
    s_c = jnp.where(cnt_c > 0, jnp.dot(qexp, kc_ref[0].astype(BF16), preferred_element_type=F32) * scale - bias_c,
                    -jnp.inf)
    s_n = jnp.where(cnt_n > 0, nt(qexp, kn.astype(BF16)) * scale - bias_n, -jnp.inf)
    m = jnp.maximum(jnp.max(s_c, axis=-1, keepdims=True), jnp.max(s_n, axis=-1, keepdims=True))
    e_c = cnt_c * jnp.exp(s_c - m)
    e_n = cnt_n * jnp.exp(s_n - m)
    den = jnp.sum(e_c, axis=-1, keepdims=True) + jnp.sum(e_n, axis=-1, keepdims=True)
    acc = (nt(e_c.astype(BF16), vc_ref[0].astype(BF16)) + jnp.dot(e_n, vn, preferred_element_type=F32)) / den
    acc = jnp.where(rh == ch, acc, 0.0)
    out = acc[:t_len]
    for h in range(1, n_heads):
        out = out + acc[h * t_len:(h + 1) * t_len]
    o_ref[:, 0, 0, :] = out


def _attn_sample(q, k, v, kc, vc, t_len, seqs, n_heads):
    _, w, win = kc.shape
    tok = pl.BlockSpec((t_len, 1, 1, w), lambda b: (0, b, 0, 0))
    cache = pl.BlockSpec((1, w, win), lambda b: (b, 0, 0))
    v4 = lambda a: a.reshape(t_len, seqs, 1, w)
    o, ko, vo = pl.pallas_call(
        functools.partial(_attn_sample_kernel, n_heads=n_heads),
        grid=(seqs,),
        in_specs=[tok, tok, tok, cache, cache],
        out_specs=[tok, cache, cache],
        out_shape=[jax.ShapeDtypeStruct((t_len, seqs, 1, w), F32)] + [jax.ShapeDtypeStruct(kc.shape, F32)] * 2,
        compiler_params=_cparams(1),
        name="attn_sample",
    )(v4(q), v4(k), v4(v), kc, vc)
    return o.reshape(t_len * seqs, w), ko, vo


def _post_kernel(x_ref, y_ref, g_ref, bonus_ref, at_ref, lnw_ref, lnb_ref, gain_ref, wout_ref, nffn_ref, wr_hi_ref,
                 wr_lo_ref, br_ref, bd_ref, x1_ref, hn_ref, comb_ref, sel_ref, *, attn_slabs, rw):

    bd = bd_ref[...]
    y = y_ref[...]
    mean = _hilo_dot(y, bd) * (1.0 / HEAD_DIM)
    d = y - mean
    var = _hilo_dot(d * d, bd) * (1.0 / HEAD_DIM)
    yn = d * lax.rsqrt(var + GN_EPS) * lnw_ref[...] + lnb_ref[...]
    rw_out = (yn + bonus_ref[...]) * g_ref[...]

    if attn_slabs:
        at = jnp.concatenate([at_ref[s] for s in range(at_ref.shape[0])], axis=1)
    else:
        at = at_ref[...]
    at_out = _rms(at, gain_ref[...])

    x1 = (x_ref[...] + jnp.dot(rw_out.astype(BF16), wout_ref[:rw, :], preferred_element_type=F32)
          + jnp.dot(at_out.astype(BF16), wout_ref[rw:, :], preferred_element_type=F32))
    x1_ref[...] = x1
    hn = _rms(x1, nffn_ref[...])
    hn_ref[...] = hn.astype(BF16)

    hi, lo = _split2(hn)
    logits = (jnp.dot(hi, wr_hi_ref[...], preferred_element_type=F32)
              + jnp.dot(lo, wr_hi_ref[...], preferred_element_type=F32)
              + jnp.dot(hi, wr_lo_ref[...], preferred_element_type=F32)) + br_ref[...]
    lane = lax.broadcasted_iota(jnp.int32, logits.shape, 1)
    big = jnp.int32(ROUTER_LANES)
    first_max = lambda z: jnp.min(jnp.where(z == jnp.max(z, axis=-1, keepdims=True), lane, big), axis=-1, keepdims=True)
    gl = jnp.where(lane < N_GROUPS, logits, -jnp.inf)
    gsel = first_max(gl)
    gw = 1.0 / jnp.sum(jnp.exp(gl - jnp.max(gl, axis=-1, keepdims=True)), axis=-1, keepdims=True)
    e_lane = lane - EXPERT_LANE0
    in_group = (e_lane >= 0) & (e_lane < N_EXPERTS) & (e_lane // EXPERTS_PER_GROUP == gsel)
    el = jnp.where(in_group, logits, -jnp.inf)
    m1 = jnp.max(el, axis=-1, keepdims=True)
    i1 = first_max(el)
    el2 = jnp.where(lane == i1, -jnp.inf, el)
    m2 = jnp.max(el2, axis=-1, keepdims=True)
    i2 = first_max(el2)
    e2 = jnp.exp(m2 - m1)
    w1 = gw / (1.0 + e2)
    comb_ref[...] = jnp.where(lane == i1, w1, 0.0) + jnp.where(lane == i2, w1 * e2, 0.0)
    sel_ref[...] = jnp.where((lane == i1) | (lane == i2), 1.0, 0.0)


def _post(x, y, g, bonus, attn, prm, rw, tm):
    n, d = x.shape
    row = lambda w: pl.BlockSpec((tm, w), lambda i: (i, 0))
    attn_slabs = attn.ndim == 3
    at_spec = pl.BlockSpec((attn.shape[0], tm, PAIR), lambda i: (0, i, 0)) if attn_slabs else row(attn.shape[1])
    params = [prm[k] for k in ("ln_x_w", "ln_x_b", "attn_gain", "w_out", "norm_ffn", "wr_hi", "wr_lo", "b_router", "bd")]
    return pl.pallas_call(
        functools.partial(_post_kernel, attn_slabs=attn_slabs, rw=rw),
        grid=(n // tm,),
        in_specs=[row(d), row(rw), row(rw), row(rw), at_spec] + [_full(p.shape) for p in params],
        out_specs=[row(d), row(d), row(ROUTER_LANES), row(ROUTER_LANES)],
        out_shape=[jax.ShapeDtypeStruct((n, d), F32), jax.ShapeDtypeStruct((n, d), BF16)]
                  + [jax.ShapeDtypeStruct((n, ROUTER_LANES), F32)] * 2,
        compiler_params=_cparams(1),
        name="post",
    )(x, y, g, bonus, attn, *params)


MOE_ROW_BLOCK = 128
MOE_SEG_ALIGN = 16
TOP_K = 2


def _moe_rows(tm):
    rows = TOP_K * tm + N_EXPERTS * (MOE_SEG_ALIGN - 1)
    return -(-rows // LANES) * LANES


def _moe_kernel(hn_ref, comb_ref, sel_ref, tri_ref, wg_ref, wu_ref, wd_ref, o_ref, p_all, xs, ys, cs, meta):
    e = pl.program_id(1)
    tm = hn_ref.shape[0]
    rows = p_all.shape[1]
    tn = lambda a_, b_: lax.dot_general(a_, b_, (((0,), (0,)), ((), ())), preferred_element_type=F32)

    @pl.when(e == 0)
    def _():
        sel = sel_ref[...]
        picked = sel > 0.0
        rank = jnp.dot(tri_ref[...], sel.astype(BF16), preferred_element_type=F32)
        cnt = rank[tm - 1:tm] + sel[tm - 1:tm]
        seg = jnp.ceil(cnt * (1.0 / MOE_SEG_ALIGN))
        li = lax.broadcasted_iota(jnp.int32, (LANES, LANES), 0)
        lj = lax.broadcasted_iota(jnp.int32, (LANES, LANES), 1)
        before = (li < lj).astype(BF16)
        off = jnp.dot(jnp.broadcast_to(seg, (8, LANES)).astype(BF16), before,
                      preferred_element_type=F32)[0:1] * float(MOE_SEG_ALIGN)
        meta[0:1, :] = cnt
        meta[1:2, :] = off
        dest = off + rank
        d_lo = jnp.min(jnp.where(picked, dest, 1e9), axis=-1, keepdims=True)
        d_hi = jnp.max(jnp.where(picked, dest, -1.0), axis=-1, keepdims=True)
        comb = comb_ref[...]
        w_lo = jnp.sum(jnp.where(picked & (dest == d_lo), comb, 0.0), axis=-1, keepdims=True)
        w_hi = jnp.sum(jnp.where(picked & (dest == d_hi), comb, 0.0), axis=-1, keepdims=True)
        d_lo, d_hi = d_lo.astype(jnp.int32), d_hi.astype(jnp.int32)
        hn = hn_ref[...]
        chunk = next(c for c in (640, 512, 384, 256, 128) if rows % c == 0)
        for c0 in range(0, rows, chunk):
            row_id = c0 + lax.broadcasted_iota(jnp.int32, (tm, chunk), 1)
            is_lo = row_id == d_lo
            is_hi = row_id == d_hi
            one = jnp.where(is_lo | is_hi, 1.0, 0.0).astype(BF16)
            p_all[:, c0:c0 + chunk] = one
            xs[c0:c0 + chunk, :] = tn(one, hn).astype(BF16)
            crow = jnp.sum(jnp.where(is_lo, w_lo, 0.0) + jnp.where(is_hi, w_hi, 0.0), axis=0, keepdims=True)
            for k0 in range(0, chunk, LANES):
                diag = jnp.where(li == lj, jnp.broadcast_to(crow[:, k0:k0 + LANES], (LANES, LANES)), 0.0)
                cs[c0 + k0:c0 + k0 + LANES, :] = jnp.broadcast_to(jnp.sum(diag, axis=1, keepdims=True), (LANES, LANES))
        xs[rows:, :] = jnp.zeros((xs.shape[0] - rows, xs.shape[1]), BF16)
        ys[...] = jnp.zeros_like(ys)
        cs[rows:, :] = jnp.zeros((cs.shape[0] - rows, LANES), F32)

    lane = lax.broadcasted_iota(jnp.int32, (1, LANES), 1)
    mine = lane == e + EXPERT_LANE0
    n_e = jnp.sum(jnp.where(mine, meta[0:1, :], 0.0)).astype(jnp.int32)
    off_e = jnp.sum(jnp.where(mine, meta[1:2, :], 0.0)).astype(jnp.int32)

    def block(b, carry):
        r = pl.ds(pl.multiple_of(off_e + b * MOE_ROW_BLOCK, MOE_SEG_ALIGN), MOE_ROW_BLOCK)
        x = xs[r, :]
        gate = jnp.dot(x, wg_ref[0], preferred_element_type=F32)
        up = jnp.dot(x, wu_ref[0], preferred_element_type=F32)
        act = (gate * jax.nn.sigmoid(gate) * up * cs[r, 0:1]).astype(BF16)
        ys[r, :] = jnp.dot(act, wd_ref[0], preferred_element_type=F32).astype(BF16)
        return carry

    lax.fori_loop(0, (n_e + MOE_ROW_BLOCK - 1) // MOE_ROW_BLOCK, block, 0)

    @pl.when(e == pl.num_programs(1) - 1)
    def _():
        o_ref[...] = jnp.dot(p_all[...], ys[:rows, :], preferred_element_type=F32)


def _moe(hn, comb, sel, wg, wu, wd, tm):
    n, d = hn.shape
    n_exp, _, de = wg.shape
    rows = _moe_rows(tm)
    tri = jnp.asarray(np.tril(np.ones((tm, tm), np.float32), -1), BF16)
    tile = lambda w: pl.BlockSpec((tm, w), lambda i, e: (i, 0))
    return pl.pallas_call(
        _moe_kernel,
        grid=(n // tm, n_exp),
        in_specs=[tile(d), tile(ROUTER_LANES), tile(ROUTER_LANES),
                  pl.BlockSpec((tm, tm), lambda i, e: (0, 0)),
                  pl.BlockSpec((1, d, de), lambda i, e: (e, 0, 0)),
                  pl.BlockSpec((1, d, de), lambda i, e: (e, 0, 0)),
                  pl.BlockSpec((1, de, d), lambda i, e: (e, 0, 0))],
        out_specs=tile(d),
        out_shape=jax.ShapeDtypeStruct((n, d), F32),
        scratch_shapes=[pltpu.VMEM((tm, rows), BF16),
                        pltpu.VMEM((rows + MOE_ROW_BLOCK, d), BF16), pltpu.VMEM((rows + MOE_ROW_BLOCK, d), BF16),
                        pltpu.VMEM((rows + MOE_ROW_BLOCK, LANES), F32), pltpu.VMEM((8, LANES), F32)],
        compiler_params=_cparams(2),
        name="moe",
    )(hn, comb, sel, tri, wg, wu, wd)


def _tail_kernel(x1_ref, moe_ref, p_ref, nple_ref, wgate_ref, wple_ref, nfin_ref, y_ref):
    x2 = x1_ref[...] + moe_ref[...]
    gate = jax.nn.sigmoid(jnp.dot(_rms(x2, nple_ref[...]).astype(BF16), wgate_ref[...], preferred_element_type=F32))
    x3 = x2 + jnp.dot(p_ref[...].astype(BF16), wple_ref[...], preferred_element_type=F32) * gate
    y_ref[...] = _rms(x3, nfin_ref[...])


def _tail(x1, moe, p, prm, tm):
    n, d = x1.shape
    row = lambda w: pl.BlockSpec((tm, w), lambda i: (i, 0))
    params = [prm[k] for k in ("norm_ple", "w_ple_gate", "w_ple", "norm_final")]
    return pl.pallas_call(
        _tail_kernel,
        grid=(n // tm,),
        in_specs=[row(d), row(d), row(p.shape[1])] + [_full(a.shape) for a in params],
        out_specs=row(d),
        out_shape=jax.ShapeDtypeStruct((n, d), F32),
        compiler_params=_cparams(1),
        name="tail",
    )(x1, moe, p, *params)


def _layer_params(i, norm_mix, w_in, mu, w0, w2, a0, a2, g2, k_k, k_a, r_k, ln_x_w, ln_x_b, attn_gain, w_out,
                  norm_ffn, w_group, b_group, w_expert_router, b_expert_router, w_gate, w_up, w_down,
                  norm_ple, w_ple, w_ple_gate, norm_final):
    rw = w0.shape[1]
    d = w_in.shape[1]
    vec = lambda a: a[i].reshape(1, -1)
    lora_rows = DECAY_LORA + AAA_LORA + GATE_LORA

    def lora(w, lo):
        return jnp.zeros((lora_rows, rw), F32).at[lo:lo + w.shape[0]].set(w).astype(BF16)

    head = np.arange(rw) // HEAD_DIM
    w_router = jnp.zeros((d, ROUTER_LANES), F32)
    w_router = w_router.at[:, :N_GROUPS].set(w_group[i]).at[:, EXPERT_LANE0:EXPERT_LANE0 + N_EXPERTS].set(w_expert_router[i])
    wr_hi, wr_lo = _split2(w_router)
    b_router = jnp.zeros((1, ROUTER_LANES), F32)
    b_router = b_router.at[0, :N_GROUPS].set(b_group[i]).at[0, EXPERT_LANE0:EXPERT_LANE0 + N_EXPERTS].set(b_expert_router[i])
    return dict(
        norm_mix=vec(norm_mix), w_in=w_in[i].astype(BF16), mu=vec(mu), w0=vec(w0), a0=vec(a0),
        w2p=lora(w2[i], 0), a2p=lora(a2[i], DECAY_LORA), g2p=lora(g2[i], DECAY_LORA + AAA_LORA),
        k_k=vec(k_k), k_a=vec(k_a), r_k=vec(r_k), ln_x_w=vec(ln_x_w), ln_x_b=vec(ln_x_b),
        bd=jnp.asarray(head[:, None] == head[None, :], BF16),
        attn_gain=vec(attn_gain), w_out=w_out[i].astype(BF16), norm_ffn=vec(norm_ffn),
        wr_hi=wr_hi, wr_lo=wr_lo, b_router=b_router,
        w_gate=w_gate[i].astype(BF16), w_up=w_up[i].astype(BF16), w_down=w_down[i].astype(BF16),
        norm_ple=vec(norm_ple), w_ple=w_ple[i].astype(BF16), w_ple_gate=w_ple_gate[i].astype(BF16),
        norm_final=norm_final.reshape(1, -1),
    )


def _tile(n, want):
    t = min(n, want)
    assert n % t == 0, (n, t)
    return t


def _ffn_and_tail(x, y, g, bonus, attn, p, prm, rw):
    n = x.shape[0]
    x1, hn, comb, sel = _post(x, y, g, bonus, attn, prm, rw, _tile(n, 256))
    moe = _moe(hn, comb, sel, prm["w_gate"], prm["w_up"], prm["w_down"], _tile(n, 1024))
    return _tail(x1, moe, p, prm, _tile(n, 256))


def kernel(x_prompt, x_sample, state_wkv, state_shift, cache_k_win, cache_v_win, p_prompt, p_sample, norm_mix, w_in, mu, w0, w2, a0, a2, g2, k_k, k_a, r_k, ln_x_w, ln_x_b, attn_gain, w_out, norm_ffn, w_group, b_group, w_expert_router, b_expert_router, w_gate, w_up, w_down, norm_ple, w_ple, w_ple_gate, norm_final):
    depth = norm_mix.shape[0]
    assert depth == 1, "single-layer step"
    bp, s_len, d = x_prompt.shape
    bs, t_len, _ = x_sample.shape
    rw = w0.shape[1]
    n_rw_heads = rw // HEAD_DIM
    rw_cols = mu.shape[1]
    at_width = (w_in.shape[2] - rw_cols) // 3
    n_at_heads = at_width // HEAD_DIM
    win = cache_k_win.shape[2]
    assert s_len % (max(DILATIONS) * ATT_BLOCK) == 0 and s_len >= win and t_len == 8 and bs % 8 == 0
    assert all(wd // dl == ATT_BLOCK for wd, dl in zip(WINDOWS, DILATIONS)) and win >= max(WINDOWS)

    prm = _layer_params(0, norm_mix, w_in, mu, w0, w2, a0, a2, g2, k_k, k_a, r_k, ln_x_w, ln_x_b, attn_gain, w_out,
                        norm_ffn, w_group, b_group, w_expert_router, b_expert_router, w_gate, w_up, w_down,
                        norm_ple, w_ple, w_ple_gate, norm_final)

    n_p = bp * s_len
    xp = x_prompt.reshape(n_p, d)
    pr, q, k, v = _in_proj(xp, prm["norm_mix"], prm["w_in"], rw_cols, at_width, _tile(n_p, 256), True)
    tm = _tile(s_len, 512)
    r_, lw, km, vv, kk, kka, g, bonus = _rwkv_prep(pr, pr, prm, rw, tm, 1, s_len // tm)
    ct = _tile(s_len, 256)
    y, wkv_p = _wkv_prompt(bp, n_rw_heads, (r_, lw, km, vv, kk, kka), ct)
    attn = _attn_prompt(q, k, v, bp, n_at_heads)
    y_prompt = _ffn_and_tail(xp, y, g, bonus, attn, p_prompt[0].reshape(n_p, -1), prm, rw).reshape(bp, s_len, d)
    shift_p = pr.reshape(bp, s_len, rw_cols)[:, -1]
    keep = min(win, s_len)

    def window(slabs):
        z = slabs.reshape(-1, bp, s_len, 2, HEAD_DIM)[:, :, s_len - keep:]
        return jnp.transpose(z, (1, 2, 0, 3, 4)).reshape(bp, keep, n_at_heads, HEAD_DIM)

    kwin_p, vwin_p = window(k), window(v)

    n_s = bs * t_len
    xs = jnp.swapaxes(x_sample, 0, 1).reshape(n_s, d)
    ps = jnp.swapaxes(p_sample[0], 0, 1).reshape(n_s, -1)
    pr_s, q_s, k_s, v_s = _in_proj(xs, prm["norm_mix"], prm["w_in"], rw_cols, at_width, _tile(n_s, 256), False)
    r_, lw, km, vv, kk, kka, g, bonus = _rwkv_prep(pr_s, state_shift[0], prm, rw, n_s, bs, 1)
    y, wkv_s = _wkv_sample(t_len, bs, n_rw_heads, (r_, lw, km, vv, kk, kka), state_wkv[0], 4)
    to_t = lambda c: jnp.transpose(c[0], (0, 2, 3, 1)).reshape(bs, at_width, win)
    from_t = lambda c: jnp.transpose(c.reshape(bs, n_at_heads, HEAD_DIM, win), (0, 3, 1, 2))[None]
    at_s, kwin_s, vwin_s = _attn_sample(q_s, k_s, v_s, to_t(cache_k_win), to_t(cache_v_win), t_len, bs, n_at_heads)
    y_s = _ffn_and_tail(xs, y, g, bonus, at_s, ps, prm, rw)
    y_sample = jnp.swapaxes(y_s.reshape(t_len, bs, d), 0, 1)
    shift_s = pr_s[(t_len - 1) * bs:]

    return (y_prompt, y_sample, wkv_p[None], shift_p[None], kwin_p[None], vwin_p[None],
            wkv_s[None], shift_s[None], from_t(kwin_s), from_t(vwin_s))
```

```python
import functools

import numpy as np
import jax
import jax.numpy as jnp
from jax import lax
from jax.experimental import pallas as pl
from jax.experimental.pallas import tpu as pltpu

F32 = jnp.float32
BF16 = jnp.bfloat16

LANES = 128
HEAD_DIM = 64
PAIR = 2 * HEAD_DIM
DECAY_LORA = 32
AAA_LORA = 32
GATE_LORA = 64
GN_EPS = 64e-5
RMS_EPS = 1e-6
WINDOWS = (128, 512, 2048)
DILATIONS = (1, 4, 16)
ATT_BLOCK = 128
N_GROUPS = 4
EXPERTS_PER_GROUP = 8
N_EXPERTS = N_GROUPS * EXPERTS_PER_GROUP
ROUTER_LANES = 128
EXPERT_LANE0 = N_GROUPS
WKV_CHUNK = 64
VMEM_LIMIT = 56 * 1024 * 1024


def _cparams(n_axes):
    return pltpu.CompilerParams(dimension_semantics=("arbitrary",) * n_axes, vmem_limit_bytes=VMEM_LIMIT)


def _full(shape):
    nd = len(shape)
    return pl.BlockSpec(shape, lambda *_: (0,) * nd)


def _bdot(a, b):
    return jnp.dot(a.astype(BF16), b.astype(BF16), preferred_element_type=F32)


def _split2(a):
    hi = a.astype(BF16)
    lo = (a - hi.astype(F32)).astype(BF16)
    return hi, lo


def _hilo_dot(a, b_bf16):
    hi, lo = _split2(a)
    return (jnp.dot(hi, b_bf16, preferred_element_type=F32)
            + jnp.dot(lo, b_bf16, preferred_element_type=F32))


def _rms(x, g):
    return x * lax.rsqrt(jnp.mean(x * x, axis=-1, keepdims=True) + RMS_EPS) * g


def _in_proj_kernel(x_ref, g_ref, w_ref, pr_ref, q_ref, k_ref, v_ref, *, rw_cols, at_width, slabs):
    h = _rms(x_ref[...], g_ref[...]).astype(BF16)
    pr_ref[...] = jnp.dot(h, w_ref[:, :rw_cols], preferred_element_type=F32)
    for i, o_ref in enumerate((q_ref, k_ref, v_ref)):
        lo = rw_cols + i * at_width
        z = jnp.dot(h, w_ref[:, lo:lo + at_width], preferred_element_type=F32)
        if slabs:
            for s in range(at_width // PAIR):
                o_ref[s] = z[:, s * PAIR:(s + 1) * PAIR]
        else:
            o_ref[...] = z


def _in_proj(x, g, w_bf16, rw_cols, at_width, tm, slabs):
    n, d = x.shape
    row = lambda w: pl.BlockSpec((tm, w), lambda i: (i, 0))
    if slabs:
        qkv_spec = pl.BlockSpec((at_width // PAIR, tm, PAIR), lambda i: (0, i, 0))
        qkv_shape = jax.ShapeDtypeStruct((at_width // PAIR, n, PAIR), F32)
    else:
        qkv_spec, qkv_shape = row(at_width), jax.ShapeDtypeStruct((n, at_width), F32)
    return pl.pallas_call(
        functools.partial(_in_proj_kernel, rw_cols=rw_cols, at_width=at_width, slabs=slabs),
        grid=(n // tm,),
        in_specs=[row(d), _full(g.shape), _full(w_bf16.shape)],
        out_specs=[row(rw_cols)] + [qkv_spec] * 3,
        out_shape=[jax.ShapeDtypeStruct((n, rw_cols), F32)] + [qkv_shape] * 3,
        compiler_params=_cparams(1),
        name="in_proj",
    )(x, g, w_bf16)


def _rwkv_prep_kernel(pr_ref, bnd_ref, mu_ref, w0_ref, w2_ref, a0_ref, a2_ref, g2_ref, kk_ref, ka_ref, rk_ref,
                      bd_ref, r_o, lw_o, k_o, v_o, kk_o, kka_o, g_o, bonus_o, *, rw, shift_rows, tiles_per_seq):
    pr = pr_ref[...]
    tm = pr.shape[0]
    if shift_rows == 1:
        first = pl.program_id(0) % tiles_per_seq == 0
        prev_row = jnp.where(first, 0.0, bnd_ref[7:8, :])
        row = lax.broadcasted_iota(jnp.int32, pr.shape, 0)
        prev = jnp.where(row == 0, prev_row, pltpu.roll(pr, 1, 0))
    else:
        prev = jnp.concatenate([bnd_ref[...], pr[:tm - shift_rows]], axis=0)
    xm = pr + (prev - pr) * mu_ref[...]
    xr, xk, xv, xl = xm[:, :rw], xm[:, rw:2 * rw], xm[:, 2 * rw:3 * rw], xm[:, 3 * rw:]
    bd = bd_ref[...]
    z = -(w0_ref[...] + _bdot(jnp.tanh(xl), w2_ref[...]))
    softplus = jnp.maximum(z, 0.0) + jnp.log(1.0 + jnp.exp(-jnp.abs(z)))
    lw = -jnp.exp(-softplus - 0.5)
    a = jax.nn.sigmoid(a0_ref[...] + _bdot(xl, a2_ref[...]))
    g = _bdot(jax.nn.sigmoid(xl), g2_ref[...])
    kk = xk * kk_ref[...]
    kk = kk / jnp.maximum(jnp.sqrt(_hilo_dot(kk * kk, bd)), 1e-12)
    kmod = xk * (1.0 + (a - 1.0) * ka_ref[...])
    bonus = _hilo_dot(xr * kmod * rk_ref[...], bd) * xv
    r_o[...] = xr
    lw_o[...] = lw
    k_o[...] = kmod
    v_o[...] = xv
    kk_o[...] = kk
    kka_o[...] = kk * a
    g_o[...] = g
    bonus_o[...] = bonus


def _rwkv_prep(pr, bnd, prm, rw, tm, shift_rows, tiles_per_seq):
    n, c = pr.shape
    row = lambda w: pl.BlockSpec((tm, w), lambda i: (i, 0))
    if shift_rows == 1:
        bnd_spec = pl.BlockSpec((8, c), lambda i: (jnp.maximum(i * (tm // 8) - 1, 0), 0))
    else:
        bnd_spec = _full(bnd.shape)
    params = [prm[k] for k in ("mu", "w0", "w2p", "a0", "a2p", "g2p", "k_k", "k_a", "r_k", "bd")]
    return pl.pallas_call(
        functools.partial(_rwkv_prep_kernel, rw=rw, shift_rows=shift_rows, tiles_per_seq=tiles_per_seq),
        grid=(n // tm,),
        in_specs=[row(c), bnd_spec] + [_full(p.shape) for p in params],
        out_specs=[row(rw)] * 8,
        out_shape=[jax.ShapeDtypeStruct((n, rw), F32)] * 8,
        compiler_params=_cparams(1),
        name="rwkv_prep",
    )(pr, bnd, *params)


def _cumsum_rows(x):
    n = x.shape[0]
    row = lax.broadcasted_iota(jnp.int32, x.shape, 0)
    s = 1
    while s < n:
        x = x + jnp.where(row >= s, pltpu.roll(x, s, 0), 0.0)
        s *= 2
    return x


def _lockstep(gens):
    results = [None] * len(gens)
    live = list(range(len(gens)))
    while live:
        still = []
        for i in live:
            try:
                next(gens[i])
                still.append(i)
            except StopIteration as stop:
                results[i] = stop.value
        live = still
    return results


def _wkv_head_chunk(pt, rt, kt, qt, kt_end, qt_end, vh, p_end, s_h, masks):
    strict, incl, eye = masks
    c = pt.shape[0]
    nt = lambda a_, b_: lax.dot_general(a_, b_, (((1,), (1,)), ((), ())), preferred_element_type=F32)
    nn = lambda a_, b_: jnp.dot(a_, b_, preferred_element_type=F32)
    tn = lambda a_, b_: lax.dot_general(a_, b_, (((0,), (0,)), ((), ())), preferred_element_type=F32)
    lhs = jnp.concatenate([pt, rt], axis=0)
    gq = nt(lhs, qt)
    gk = nt(lhs, kt)
    yield
    l_qp = jnp.where(strict, gq[:c], 0.0)
    l_kp = jnp.where(strict, gk[:c], 0.0)
    a_rq = jnp.where(incl, gq[c:], 0.0)
    a_rk = jnp.where(incl, gk[c:], 0.0)
    t_inv = eye - l_qp
    pw = -l_qp
    lv_av = nn(jnp.concatenate([l_kp, a_rk], axis=0), vh)
    for _ in range(int(np.log2(c)) - 1):
        pw = nn(pw, pw)
        yield
        t_new = nn(pw, t_inv)
        yield
        t_inv = t_inv + t_new
    w_mat = nn(t_inv, pt)
    u_v = nn(t_inv, lv_av[:c])
    yield
    ws = nt(jnp.concatenate([w_mat, rt], axis=0), s_h)
    yield
    u = u_v + ws[:c]
    au = nn(a_rq, u)
    upd = tn(jnp.concatenate([vh, -u], axis=0), jnp.concatenate([kt_end, qt_end], axis=0))
    yield
    return ws[c:] + lv_av[c:] - au, s_h * p_end + upd


def _pair_stack(z, first):
    return jnp.concatenate([jnp.where(first, z, 0.0), jnp.where(first, 0.0, z)], axis=0).astype(BF16)


def _wkv_pair_static(slabs, masks):
    r, lw, k, v, kk, kka = slabs
    first, strict, incl, eye = masks
    c = r.shape[0]
    n2 = 2 * c
    nt = lambda a_, b_: lax.dot_general(a_, b_, (((1,), (1,)), ((), ())), preferred_element_type=F32)
    nn = lambda a_, b_: jnp.dot(a_, b_, preferred_element_type=F32)
    cum = _cumsum_rows(lw)
    cum_last = cum[c - 1:c, :]
    d_end = jnp.exp(cum_last - cum)
    p_inv = jnp.exp(-cum)
    ps = _pair_stack(kk * jnp.exp(cum - lw), first)
    rs = _pair_stack(r * jnp.exp(cum), first)
    qs = _pair_stack(kka * p_inv, first)
    ks = _pair_stack(k * p_inv, first)
    vs = _pair_stack(v, first)
    ends = jnp.concatenate([_pair_stack(k * d_end, first), _pair_stack(kka * d_end, first)], axis=0)
    g = nt(jnp.concatenate([ps, rs], axis=0), jnp.concatenate([qs, ks], axis=0))
    yield
    l_qp = jnp.where(strict, g[:n2, :n2], 0.0)
    l_kp = jnp.where(strict, g[:n2, n2:], 0.0).astype(BF16)
    a_rq = jnp.where(incl, g[n2:, :n2], 0.0).astype(BF16)
    a_rk = jnp.where(incl, g[n2:, n2:], 0.0).astype(BF16)
    t_inv = eye - l_qp
    pw = (-l_qp).astype(BF16)
    lv_av = nn(jnp.concatenate([l_kp, a_rk], axis=0), vs)
    pw2 = nn(pw, pw)
    yield
    lv, av = lv_av[:n2].astype(BF16), lv_av[n2:]
    pw = pw2.astype(BF16)
    n_it = int(np.log2(c)) - 1
    for i in range(n_it):
        if i < n_it - 1:
            out = nn(pw, jnp.concatenate([pw, t_inv.astype(BF16)], axis=1))
            yield
            pw = out[:, :n2].astype(BF16)
            t_inv = t_inv + out[:, n2:]
        else:
            out = nn(pw, t_inv.astype(BF16))
            yield
            t_inv = t_inv + out
    tp = nn(t_inv.astype(BF16), jnp.concatenate([ps, lv], axis=1))
    yield
    return dict(w=tp[:, :PAIR].astype(BF16), u_v=tp[:, PAIR:], rs=rs, vs=vs, a_rq=a_rq, av=av, ends=ends,
                p_end=jnp.exp(cum_last))


def _wkv_pair_step(st, s_bd):
    nt = lambda a_, b_: lax.dot_general(a_, b_, (((1,), (1,)), ((), ())), preferred_element_type=F32)
    tn = lambda a_, b_: lax.dot_general(a_, b_, (((0,), (0,)), ((), ())), preferred_element_type=F32)
    n2 = st["w"].shape[0]
    ws = nt(jnp.concatenate([st["w"], st["rs"]], axis=0), s_bd.astype(BF16))
    yield
    u = st["u_v"] + ws[:n2]
    upd = tn(jnp.concatenate([st["vs"], (-u).astype(BF16)], axis=0), st["ends"])
    au = jnp.dot(st["a_rq"], u.astype(BF16), preferred_element_type=F32)
    yield
    y_bd = ws[n2:] + st["av"] - au
    return y_bd[:n2 // 2] + y_bd[n2 // 2:], s_bd * st["p_end"] + upd


def _wkv_prompt_kernel(r_ref, lw_ref, k_ref, v_ref, kk_ref, kka_ref, y_ref, sT_ref, s_scr, *, chunks, tiles_per_seq):
    j = pl.program_id(0) % tiles_per_seq
    n_pairs = s_scr.shape[0]
    c = WKV_CHUNK

    @pl.when(j == 0)
    def _():
        s_scr[...] = jnp.zeros_like(s_scr)

    lane = lax.broadcasted_iota(jnp.int32, (c, PAIR), 1)
    ri = lax.broadcasted_iota(jnp.int32, (2 * c, 2 * c), 0)
    ci = lax.broadcasted_iota(jnp.int32, (2 * c, 2 * c), 1)
    rt_, ct_ = ri % c, ci % c
    masks = (lane < HEAD_DIM, rt_ > ct_, rt_ >= ct_, (ri == ci).astype(F32))

    gens = []
    for ch in range(chunks):
        rows = slice(ch * c, (ch + 1) * c)
        for p in range(n_pairs):
            lanes = slice(p * PAIR, (p + 1) * PAIR)
            slabs = [ref[rows, lanes] for ref in (r_ref, lw_ref, k_ref, v_ref, kk_ref, kka_ref)]
            gens.append(_wkv_pair_static(slabs, masks))
    static = _lockstep(gens)
    states = [s_scr[p] for p in range(n_pairs)]
    for ch in range(chunks):
        outs = _lockstep([_wkv_pair_step(static[ch * n_pairs + p], states[p]) for p in range(n_pairs)])
        for p, (y, s_new) in enumerate(outs):
            states[p] = s_new
            y_ref[ch * c:(ch + 1) * c, p * PAIR:(p + 1) * PAIR] = y
    for p in range(n_pairs):
        s_scr[p] = states[p]

    @pl.when(j == tiles_per_seq - 1)
    def _():
        for p in range(n_pairs):
            sT_ref[0, 2 * p] = s_scr[p, :HEAD_DIM, :HEAD_DIM]
            sT_ref[0, 2 * p + 1] = s_scr[p, HEAD_DIM:, HEAD_DIM:]


def _wkv_prompt(seqs, n_heads, arrays, ct):
    n, w = arrays[0].shape
    tiles_per_seq = n // seqs // ct
    row = pl.BlockSpec((ct, w), lambda i: (i, 0))
    st_shape = (seqs, n_heads, HEAD_DIM, HEAD_DIM)
    return pl.pallas_call(
        functools.partial(_wkv_prompt_kernel, chunks=ct // WKV_CHUNK, tiles_per_seq=tiles_per_seq),
        grid=(n // ct,),
        in_specs=[row] * 6,
        out_specs=[row, pl.BlockSpec((1,) + st_shape[1:], lambda i: (i // tiles_per_seq, 0, 0, 0))],
        out_shape=[jax.ShapeDtypeStruct((n, w), F32), jax.ShapeDtypeStruct(st_shape, F32)],
        scratch_shapes=[pltpu.VMEM((n_heads // 2, PAIR, PAIR), F32)],
        compiler_params=_cparams(1),
        name="wkv_prompt",
    )(*arrays)


def _wkv_sample_kernel(r_ref, lw_ref, k_ref, v_ref, kk_ref, kka_ref, s0_ref, y_ref, sT_ref, *, seqs_per_step):
    n_heads = s0_ref.shape[1]
    c = r_ref.shape[0]
    ri = lax.broadcasted_iota(jnp.int32, (c, c), 0)
    ci = lax.broadcasted_iota(jnp.int32, (c, c), 1)
    masks = (ri > ci, ri >= ci, (ri == ci).astype(F32))
    gens = []
    for s in range(seqs_per_step):
        r, lw, k, v, kk, kka = [ref[:, s, 0, :] for ref in (r_ref, lw_ref, k_ref, v_ref, kk_ref, kka_ref)]
        cum = _cumsum_rows(lw)
        cum_last = cum[c - 1:c, :]
        p_inv = jnp.exp(-cum)
        d_end = jnp.exp(cum_last - cum)
        ops = (kk * jnp.exp(cum - lw), r * jnp.exp(cum), k * p_inv, kka * p_inv, k * d_end, kka * d_end, v,
               jnp.exp(cum_last))
        for h in range(n_heads):
            hs = slice(h * HEAD_DIM, (h + 1) * HEAD_DIM)
            gens.append(_wkv_head_chunk(*[z[:, hs] for z in ops], s0_ref[s, h], masks))
    outs = _lockstep(gens)
    for s in range(seqs_per_step):
        heads = outs[s * n_heads:(s + 1) * n_heads]
        y_ref[:, s, 0, :] = jnp.concatenate([y for y, _ in heads], axis=1)
        for h, (_, s_new) in enumerate(heads):
            sT_ref[s, h] = s_new


def _wkv_sample(t_len, seqs, n_heads, arrays, s0, seqs_per_step):
    w = arrays[0].shape[-1]
    arrays = [a.reshape(t_len, seqs, 1, w) for a in arrays]
    tok = pl.BlockSpec((t_len, seqs_per_step, 1, w), lambda i: (0, i, 0, 0))
    st = pl.BlockSpec((seqs_per_step, n_heads, HEAD_DIM, HEAD_DIM), lambda i: (i, 0, 0, 0))
    y, s_t = pl.pallas_call(
        functools.partial(_wkv_sample_kernel, seqs_per_step=seqs_per_step),
        grid=(seqs // seqs_per_step,),
        in_specs=[tok] * 6 + [st],
        out_specs=[tok, st],
        out_shape=[jax.ShapeDtypeStruct((t_len, seqs, 1, w), F32), jax.ShapeDtypeStruct(s0.shape, F32)],
        compiler_params=_cparams(1),
        name="wkv_sample",
    )(*arrays, s0)
    return y.reshape(t_len * seqs, w), s_t


ATT_SPAN = max(DILATIONS) * ATT_BLOCK


def _attn_prompt_kernel(q_ref, kp_ref, kc_ref, vp_ref, vc_ref, o_ref, kx, vx, ob, lb, bias_scr, *, n_heads,
                        tiles_per_iter):
    slab, blk = pl.program_id(1), pl.program_id(2)
    span = ATT_SPAN
    kx[:span] = kp_ref[0]
    kx[span:] = kc_ref[0]
    vx[:span] = vp_ref[0]
    vx[span:] = vc_ref[0]
    qi = lax.broadcasted_iota(jnp.int32, (ATT_BLOCK, 2 * ATT_BLOCK), 0)
    ki = lax.broadcasted_iota(jnp.int32, (ATT_BLOCK, 2 * ATT_BLOCK), 1)
    steps = ATT_BLOCK + qi - ki
    band = (steps >= 0) & (steps <= ATT_BLOCK)
    stepsf = steps.astype(F32)
    first = lax.broadcasted_iota(jnp.int32, (ATT_BLOCK, PAIR), 1) < HEAD_DIM
    log2e = float(np.log2(np.e))
    scale = HEAD_DIM ** -0.5 * log2e
    for h2 in range(2):
        alibi = jnp.exp2(jnp.zeros_like(stepsf) - (8.0 / n_heads) * (2 * slab + 1 + h2).astype(F32)) * (stepsf * log2e)
        for br, dil in enumerate(DILATIONS):
            bias_scr[0, br, h2] = jnp.where(band, alibi * float(dil), jnp.inf)
            bias_scr[1, br, h2] = jnp.where(band & (ki >= ATT_BLOCK), alibi * float(dil), jnp.inf)

    def head(q_h, k_t, v_t, bias):
        s = lax.dot_general(q_h, k_t, (((1,), (1,)), ((), ())), preferred_element_type=F32)
        yield
        s = s - bias
        m = jnp.max(s, axis=-1, keepdims=True)
        e = jnp.exp2(s - m)
        den = jnp.sum(e, axis=-1, keepdims=True)
        o = jnp.dot(e.astype(BF16), v_t, preferred_element_type=F32)
        yield
        return o / den, m + jnp.log2(den)

    def body(it, carry):
        gens, where_to = [], []
        for u in range(tiles_per_iter):
            t = it * tiles_per_iter + u
            for br, dil in enumerate(DILATIONS):
                sub = t // dil
                base = t % dil + dil * ATT_BLOCK * sub
                rows_q = pl.ds(base, ATT_BLOCK, stride=dil)
                rows_kv = pl.ds(span + base - dil * ATT_BLOCK, 2 * ATT_BLOCK, stride=dil)
                q_t = q_ref[0, rows_q, :] * scale
                k_t = kx[rows_kv, :].astype(BF16)
                v_t = vx[rows_kv, :].astype(BF16)
                at_start = jnp.where((blk > 0) | (sub > 0), 0, 1)
                for h2 in range(2):
                    q_h = jnp.where(first if h2 == 0 else ~first, q_t, 0.0).astype(BF16)
                    gens.append(head(q_h, k_t, v_t, bias_scr[at_start, br, h2]))
                where_to.append((br, rows_q))
        res = _lockstep(gens)
        for i, (br, rows_q) in enumerate(where_to):
            (o0, l0), (o1, l1) = res[2 * i], res[2 * i + 1]
            ob[br, rows_q, :] = jnp.where(first, o0, o1)
            lb[br, rows_q, :] = jnp.where(first, l0, l1)
        return carry

    lax.fori_loop(0, span // ATT_BLOCK // tiles_per_iter, body, 0)
    ls = [lb[br] for br in range(len(DILATIONS))]
    m = functools.reduce(jnp.maximum, ls)
    es = [jnp.exp2(l - m) for l in ls]
    o_ref[0] = sum(e * ob[br] for br, e in enumerate(es)) / sum(es)


def _attn_prompt(q, k, v, seqs, n_heads):
    n_slabs, n, _ = q.shape
    n_blk = n // seqs // ATT_SPAN
    cur = pl.BlockSpec((1, ATT_SPAN, PAIR), lambda b, s, i: (s, b * n_blk + i, 0))
    prev = pl.BlockSpec((1, ATT_SPAN, PAIR), lambda b, s, i: (s, b * n_blk + jnp.maximum(i - 1, 0), 0))
    return pl.pallas_call(
        functools.partial(_attn_prompt_kernel, n_heads=n_heads, tiles_per_iter=2),
        grid=(seqs, n_slabs, n_blk),
        in_specs=[cur, prev, cur, prev, cur],
        out_specs=cur,
        out_shape=jax.ShapeDtypeStruct(q.shape, F32),
        scratch_shapes=[pltpu.VMEM((2 * ATT_SPAN, PAIR), F32)] * 2
                       + [pltpu.VMEM((len(DILATIONS), ATT_SPAN, PAIR), F32)] * 2
                       + [pltpu.VMEM((2, len(DILATIONS), 2, ATT_BLOCK, 2 * ATT_BLOCK), F32)],
        compiler_params=_cparams(3),
        name="attn_prompt",
    )(q, k, k, v, v)


def _attn_sample_kernel(q_ref, kn_ref, vn_ref, kc_ref, vc_ref, o_ref, ko_ref, vo_ref, *, n_heads):
    t_len = q_ref.shape[0]
    w, win = kc_ref.shape[1], kc_ref.shape[2]
    rows = n_heads * t_len
    q = q_ref[:, 0, 0, :]
    kn = kn_ref[:, 0, 0, :]
    vn = vn_ref[:, 0, 0, :]
    nt = lambda a_, b_: lax.dot_general(a_, b_, (((1,), (1,)), ((), ())), preferred_element_type=F32)
    tn = lambda a_, b_: lax.dot_general(a_, b_, (((0,), (0,)), ((), ())), preferred_element_type=F32)

    pad_rows = 16
    place = (lax.broadcasted_iota(jnp.int32, (pad_rows, LANES), 1)
             == lax.broadcasted_iota(jnp.int32, (pad_rows, LANES), 0) + (LANES - t_len)).astype(BF16)
    lane = lax.broadcasted_iota(jnp.int32, (LANES, LANES), 1)

    def transposed_tail(z):
        zp = jnp.concatenate([z, jnp.zeros((pad_rows - t_len, w), F32)], axis=0)
        hi = zp.astype(BF16)
        r1 = zp - hi.astype(F32)
        mid = r1.astype(BF16)
        lo = (r1 - mid.astype(F32)).astype(BF16)
        return tn(hi, place) + tn(mid, place) + tn(lo, place)

    for src, new, dst in ((kc_ref, kn, ko_ref), (vc_ref, vn, vo_ref)):
        tail = transposed_tail(new)
        for r0 in range(0, w, LANES):
            rolled = pltpu.roll(src[0, r0:r0 + LANES, :], win - t_len, 1)
            dst[0, r0:r0 + LANES, :win - LANES] = rolled[:, :win - LANES]
            dst[0, r0:r0 + LANES, win - LANES:] = jnp.where(lane < LANES - t_len, rolled[:, win - LANES:],
                                                             tail[r0:r0 + LANES])

    qrep = jnp.concatenate([q] * n_heads, axis=0)
    rh = lax.broadcasted_iota(jnp.int32, (rows, w), 0) // t_len
    ch = lax.broadcasted_iota(jnp.int32, (rows, w), 1) // HEAD_DIM
    qexp = jnp.where(rh == ch, qrep, 0.0).astype(BF16)
    scale = HEAD_DIM ** -0.5

    def weights(n_keys, key_pos0):
        ri = lax.broadcasted_iota(jnp.int32, (rows, n_keys), 0)
        ki = lax.broadcasted_iota(jnp.int32, (rows, n_keys), 1)
        dist = win + ri % t_len - (key_pos0 + ki)
        cnt = jnp.zeros((rows, n_keys), F32)
        for wdw, dil in zip(WINDOWS, DILATIONS):
            ok = (dist >= 0) & (dist % dil == 0) & (dist <= wdw)
            cnt = cnt + jnp.where(ok, 1.0, 0.0)
        slope = jnp.exp2(-8.0 * (ri // t_len + 1).astype(F32) / n_heads)
        return cnt, slope * dist.astype(F32)

    cnt_c, bias_c = weights(win, 0)
    cnt_n, bias_n = weights(t_len, win)
    s_c = jnp.where(cnt_c > 0, jnp.dot(qexp, kc_ref[0].astype(BF16), preferred_element_type=F32) * scale - bias_c,
                    -jnp.inf)
    s_n = jnp.where(cnt_n > 0, nt(qexp, kn.astype(BF16)) * scale - bias_n, -jnp.inf)
    m = jnp.maximum(jnp.max(s_c, axis=-1, keepdims=True), jnp.max(s_n, axis=-1, keepdims=True))
    e_c = cnt_c * jnp.exp(s_c - m)
    e_n = cnt_n * jnp.exp(s_n - m)
    den = jnp.sum(e_c, axis=-1, keepdims=True) + jnp.sum(e_n, axis=-1, keepdims=True)
    acc = (nt(e_c.astype(BF16), vc_ref[0].astype(BF16)) + jnp.dot(e_n, vn, preferred_element_type=F32)) / den
    acc = jnp.where(rh == ch, acc, 0.0)
    out = acc[:t_len]
    for h in range(1, n_heads):
        out = out + acc[h * t_len:(h + 1) * t_len]
    o_ref[:, 0, 0, :] = out


def _attn_sample(q, k, v, kc, vc, t_len, seqs, n_heads):
    _, w, win = kc.shape
    tok = pl.BlockSpec((t_len, 1, 1, w), lambda b: (0, b, 0, 0))
    cache = pl.BlockSpec((1, w, win), lambda b: (b, 0, 0))
    v4 = lambda a: a.reshape(t_len, seqs, 1, w)
    o, ko, vo = pl.pallas_call(
        functools.partial(_attn_sample_kernel, n_heads=n_heads),
        grid=(seqs,),
        in_specs=[tok, tok, tok, cache, cache],
        out_specs=[tok, cache, cache],
        out_shape=[jax.ShapeDtypeStruct((t_len, seqs, 1, w), F32)] + [jax.ShapeDtypeStruct(kc.shape, F32)] * 2,
        compiler_params=_cparams(1),
        name="attn_sample",
    )(v4(q), v4(k), v4(v), kc, vc)
    return o.reshape(t_len * seqs, w), ko, vo


def _post_kernel(x_ref, y_ref, g_ref, bonus_ref, at_ref, lnw_ref, lnb_ref, gain_ref, wout_ref, nffn_ref, wr_hi_ref,
                 wr_lo_ref, br_ref, bd_ref, x1_ref, hn_ref, comb_ref, sel_ref, *, attn_slabs, rw):

    bd = bd_ref[...]
    y = y_ref[...]
    mean = _hilo_dot(y, bd) * (1.0 / HEAD_DIM)
    d = y - mean
    var = _hilo_dot(d * d, bd) * (1.0 / HEAD_DIM)
    yn = d * lax.rsqrt(var + GN_EPS) * lnw_ref[...] + lnb_ref[...]
    rw_out = (yn + bonus_ref[...]) * g_ref[...]

    if attn_slabs:
        at = jnp.concatenate([at_ref[s] for s in range(at_ref.shape[0])], axis=1)
    else:
        at = at_ref[...]
    at_out = _rms(at, gain_ref[...])

    x1 = (x_ref[...] + jnp.dot(rw_out.astype(BF16), wout_ref[:rw, :], preferred_element_type=F32)
          + jnp.dot(at_out.astype(BF16), wout_ref[rw:, :], preferred_element_type=F32))
    x1_ref[...] = x1
    hn = _rms(x1, nffn_ref[...])
    hn_ref[...] = hn.astype(BF16)

    hi, lo = _split2(hn)
    logits = (jnp.dot(hi, wr_hi_ref[...], preferred_element_type=F32)
              + jnp.dot(lo, wr_hi_ref[...], preferred_element_type=F32)
              + jnp.dot(hi, wr_lo_ref[...], preferred_element_type=F32)) + br_ref[...]
    lane = lax.broadcasted_iota(jnp.int32, logits.shape, 1)
    big = jnp.int32(ROUTER_LANES)
    first_max = lambda z: jnp.min(jnp.where(z == jnp.max(z, axis=-1, keepdims=True), lane, big), axis=-1, keepdims=True)
    gl = jnp.where(lane < N_GROUPS, logits, -jnp.inf)
    gsel = first_max(gl)
    gw = 1.0 / jnp.sum(jnp.exp(gl - jnp.max(gl, axis=-1, keepdims=True)), axis=-1, keepdims=True)
    e_lane = lane - EXPERT_LANE0
    in_group = (e_lane >= 0) & (e_lane < N_EXPERTS) & (e_lane // EXPERTS_PER_GROUP == gsel)
    el = jnp.where(in_group, logits, -jnp.inf)
    m1 = jnp.max(el, axis=-1, keepdims=True)
    i1 = first_max(el)
    el2 = jnp.where(lane == i1, -jnp.inf, el)
    m2 = jnp.max(el2, axis=-1, keepdims=True)
    i2 = first_max(el2)
    e2 = jnp.exp(m2 - m1)
    w1 = gw / (1.0 + e2)
    comb_ref[...] = jnp.where(lane == i1, w1, 0.0) + jnp.where(lane == i2, w1 * e2, 0.0)
    sel_ref[...] = jnp.where((lane == i1) | (lane == i2), 1.0, 0.0)


def _post(x, y, g, bonus, attn, prm, rw, tm):
    n, d = x.shape
    row = lambda w: pl.BlockSpec((tm, w), lambda i: (i, 0))
    attn_slabs = attn.ndim == 3
    at_spec = pl.BlockSpec((attn.shape[0], tm, PAIR), lambda i: (0, i, 0)) if attn_slabs else row(attn.shape[1])
    params = [prm[k] for k in ("ln_x_w", "ln_x_b", "attn_gain", "w_out", "norm_ffn", "wr_hi", "wr_lo", "b_router", "bd")]
    return pl.pallas_call(
        functools.partial(_post_kernel, attn_slabs=attn_slabs, rw=rw),
        grid=(n // tm,),
        in_specs=[row(d), row(rw), row(rw), row(rw), at_spec] + [_full(p.shape) for p in params],
        out_specs=[row(d), row(d), row(ROUTER_LANES), row(ROUTER_LANES)],
        out_shape=[jax.ShapeDtypeStruct((n, d), F32), jax.ShapeDtypeStruct((n, d), BF16)]
                  + [jax.ShapeDtypeStruct((n, ROUTER_LANES), F32)] * 2,
        compiler_params=_cparams(1),
        name="post",
    )(x, y, g, bonus, attn, *params)


MOE_ROW_BLOCK = 128
MOE_EXPERTS_PER_STEP = 2
MOE_SEG_ALIGN = 16
TOP_K = 2


def _moe_rows(tm):
    rows = TOP_K * tm + N_EXPERTS * (MOE_SEG_ALIGN - 1)
    return -(-rows // LANES) * LANES


def _moe_kernel(hn_ref, comb_ref, sel_ref, tri_ref, wg_ref, wu_ref, wd_ref, o_ref, p_all, xs, ys, cs, meta):
    e = pl.program_id(1)
    tm = hn_ref.shape[0]
    rows = p_all.shape[1]
    tn = lambda a_, b_: lax.dot_general(a_, b_, (((0,), (0,)), ((), ())), preferred_element_type=F32)

    @pl.when(e == 0)
    def _():
        sel = sel_ref[...]
        picked = sel > 0.0
        rank = jnp.dot(tri_ref[...], sel.astype(BF16), preferred_element_type=F32)
        cnt = rank[tm - 1:tm] + sel[tm - 1:tm]
        seg = jnp.ceil(cnt * (1.0 / MOE_SEG_ALIGN))
        li = lax.broadcasted_iota(jnp.int32, (LANES, LANES), 0)
        lj = lax.broadcasted_iota(jnp.int32, (LANES, LANES), 1)
        before = (li < lj).astype(BF16)
        off = jnp.dot(jnp.broadcast_to(seg, (8, LANES)).astype(BF16), before,
                      preferred_element_type=F32)[0:1] * float(MOE_SEG_ALIGN)
        meta[0:1, :] = cnt
        meta[1:2, :] = off
        dest = off + rank
        d_lo = jnp.min(jnp.where(picked, dest, 1e9), axis=-1, keepdims=True)
        d_hi = jnp.max(jnp.where(picked, dest, -1.0), axis=-1, keepdims=True)
        comb = comb_ref[...]
        w_lo = jnp.sum(jnp.where(picked & (dest == d_lo), comb, 0.0), axis=-1, keepdims=True)
        w_hi = jnp.sum(jnp.where(picked & (dest == d_hi), comb, 0.0), axis=-1, keepdims=True)
        d_lo, d_hi = d_lo.astype(jnp.int32), d_hi.astype(jnp.int32)
        hn = hn_ref[...]
        chunk = next(c for c in (256, 128) if rows % c == 0)
        for c0 in range(0, rows, chunk):
            row_id = c0 + lax.broadcasted_iota(jnp.int32, (tm, chunk), 1)
            is_lo = row_id == d_lo
            is_hi = row_id == d_hi
            one = jnp.where(is_lo | is_hi, 1.0, 0.0).astype(BF16)
            p_all[:, c0:c0 + chunk] = one
            xs[c0:c0 + chunk, :] = tn(one, hn).astype(BF16)
            crow = jnp.sum(jnp.where(is_lo, w_lo, 0.0) + jnp.where(is_hi, w_hi, 0.0), axis=0, keepdims=True)
            for k0 in range(0, chunk, LANES):
                diag = jnp.where(li == lj, jnp.broadcast_to(crow[:, k0:k0 + LANES], (LANES, LANES)), 0.0)
                cs[c0 + k0:c0 + k0 + LANES, :] = jnp.broadcast_to(jnp.sum(diag, axis=1, keepdims=True), (LANES, LANES))
        xs[rows:, :] = jnp.zeros((xs.shape[0] - rows, xs.shape[1]), BF16)
        ys[...] = jnp.zeros_like(ys)
        cs[rows:, :] = jnp.zeros((cs.shape[0] - rows, LANES), F32)

    lane = lax.broadcasted_iota(jnp.int32, (1, LANES), 1)
    for j in range(wg_ref.shape[0]):
        mine = lane == e * wg_ref.shape[0] + j + EXPERT_LANE0
        n_e = jnp.sum(jnp.where(mine, meta[0:1, :], 0.0)).astype(jnp.int32)
        off_e = jnp.sum(jnp.where(mine, meta[1:2, :], 0.0)).astype(jnp.int32)

        def block(b, carry, j=j, off_e=off_e):
            r = pl.ds(pl.multiple_of(off_e + b * MOE_ROW_BLOCK, MOE_SEG_ALIGN), MOE_ROW_BLOCK)
            x = xs[r, :]
            gate = jnp.dot(x, wg_ref[j], preferred_element_type=F32)
            up = jnp.dot(x, wu_ref[j], preferred_element_type=F32)
            act = (gate * jax.nn.sigmoid(gate) * up * cs[r, 0:1]).astype(BF16)
            ys[r, :] = jnp.dot(act, wd_ref[j], preferred_element_type=F32).astype(BF16)
            return carry

        lax.fori_loop(0, (n_e + MOE_ROW_BLOCK - 1) // MOE_ROW_BLOCK, block, 0)

    @pl.when(e == pl.num_programs(1) - 1)
    def _():
        o_ref[...] = jnp.dot(p_all[...], ys[:rows, :], preferred_element_type=F32)


def _moe(hn, comb, sel, wg, wu, wd, tm):
    n, d = hn.shape
    n_exp, _, de = wg.shape
    rows = _moe_rows(tm)
    tri = jnp.asarray(np.tril(np.ones((tm, tm), np.float32), -1), BF16)
    tile = lambda w: pl.BlockSpec((tm, w), lambda i, e: (i, 0))
    return pl.pallas_call(
        _moe_kernel,
        grid=(n // tm, n_exp // MOE_EXPERTS_PER_STEP),
        in_specs=[tile(d), tile(ROUTER_LANES), tile(ROUTER_LANES),
                  pl.BlockSpec((tm, tm), lambda i, e: (0, 0)),
                  pl.BlockSpec((MOE_EXPERTS_PER_STEP, d, de), lambda i, e: (e, 0, 0)),
                  pl.BlockSpec((MOE_EXPERTS_PER_STEP, d, de), lambda i, e: (e, 0, 0)),
                  pl.BlockSpec((MOE_EXPERTS_PER_STEP, de, d), lambda i, e: (e, 0, 0))],
        out_specs=tile(d),
        out_shape=jax.ShapeDtypeStruct((n, d), F32),
        scratch_shapes=[pltpu.VMEM((tm, rows), BF16),
                        pltpu.VMEM((rows + MOE_ROW_BLOCK, d), BF16), pltpu.VMEM((rows + MOE_ROW_BLOCK, d), BF16),
                        pltpu.VMEM((rows + MOE_ROW_BLOCK, LANES), F32), pltpu.VMEM((8, LANES), F32)],
        compiler_params=_cparams(2),
        name="moe",
    )(hn, comb, sel, tri, wg, wu, wd)


def _tail_kernel(x1_ref, moe_ref, p_ref, nple_ref, wgate_ref, wple_ref, nfin_ref, y_ref):
    x2 = x1_ref[...] + moe_ref[...]
    gate = jax.nn.sigmoid(jnp.dot(_rms(x2, nple_ref[...]).astype(BF16), wgate_ref[...], preferred_element_type=F32))
    x3 = x2 + jnp.dot(p_ref[...].astype(BF16), wple_ref[...], preferred_element_type=F32) * gate
    y_ref[...] = _rms(x3, nfin_ref[...])


def _tail(x1, moe, p, prm, tm):
    n, d = x1.shape
    row = lambda w: pl.BlockSpec((tm, w), lambda i: (i, 0))
    params = [prm[k] for k in ("norm_ple", "w_ple_gate", "w_ple", "norm_final")]
    return pl.pallas_call(
        _tail_kernel,
        grid=(n // tm,),
        in_specs=[row(d), row(d), row(p.shape[1])] + [_full(a.shape) for a in params],
        out_specs=row(d),
        out_shape=jax.ShapeDtypeStruct((n, d), F32),
        compiler_params=_cparams(1),
        name="tail",
    )(x1, moe, p, *params)


def _layer_params(i, norm_mix, w_in, mu, w0, w2, a0, a2, g2, k_k, k_a, r_k, ln_x_w, ln_x_b, attn_gain, w_out,
                  norm_ffn, w_group, b_group, w_expert_router, b_expert_router, w_gate, w_up, w_down,
                  norm_ple, w_ple, w_ple_gate, norm_final):
    rw = w0.shape[1]
    d = w_in.shape[1]
    vec = lambda a: a[i].reshape(1, -1)
    lora_rows = DECAY_LORA + AAA_LORA + GATE_LORA

    def lora(w, lo):
        return jnp.zeros((lora_rows, rw), F32).at[lo:lo + w.shape[0]].set(w).astype(BF16)

    head = np.arange(rw) // HEAD_DIM
    w_router = jnp.zeros((d, ROUTER_LANES), F32)
    w_router = w_router.at[:, :N_GROUPS].set(w_group[i]).at[:, EXPERT_LANE0:EXPERT_LANE0 + N_EXPERTS].set(w_expert_router[i])
    wr_hi, wr_lo = _split2(w_router)
    b_router = jnp.zeros((1, ROUTER_LANES), F32)
    b_router = b_router.at[0, :N_GROUPS].set(b_group[i]).at[0, EXPERT_LANE0:EXPERT_LANE0 + N_EXPERTS].set(b_expert_router[i])
    return dict(
        norm_mix=vec(norm_mix), w_in=w_in[i].astype(BF16), mu=vec(mu), w0=vec(w0), a0=vec(a0),
        w2p=lora(w2[i], 0), a2p=lora(a2[i], DECAY_LORA), g2p=lora(g2[i], DECAY_LORA + AAA_LORA),
        k_k=vec(k_k), k_a=vec(k_a), r_k=vec(r_k), ln_x_w=vec(ln_x_w), ln_x_b=vec(ln_x_b),
        bd=jnp.asarray(head[:, None] == head[None, :], BF16),
        attn_gain=vec(attn_gain), w_out=w_out[i].astype(BF16), norm_ffn=vec(norm_ffn),
        wr_hi=wr_hi, wr_lo=wr_lo, b_router=b_router,
        w_gate=w_gate[i].astype(BF16), w_up=w_up[i].astype(BF16), w_down=w_down[i].astype(BF16),
        norm_ple=vec(norm_ple), w_ple=w_ple[i].astype(BF16), w_ple_gate=w_ple_gate[i].astype(BF16),
        norm_final=norm_final.reshape(1, -1),
    )


def _tile(n, want):
    t = min(n, want)
    assert n % t == 0, (n, t)
    return t


def _ffn_and_tail(x, y, g, bonus, attn, p, prm, rw):
    n = x.shape[0]
    x1, hn, comb, sel = _post(x, y, g, bonus, attn, prm, rw, _tile(n, 256))
    moe = _moe(hn, comb, sel, prm["w_gate"], prm["w_up"], prm["w_down"], _tile(n, 1024))
    return _tail(x1, moe, p, prm, _tile(n, 256))


def kernel(x_prompt, x_sample, state_wkv, state_shift, cache_k_win, cache_v_win, p_prompt, p_sample, norm_mix, w_in, mu, w0, w2, a0, a2, g2, k_k, k_a, r_k, ln_x_w, ln_x_b, attn_gain, w_out, norm_ffn, w_group, b_group, w_expert_router, b_expert_router, w_gate, w_up, w_down, norm_ple, w_ple, w_ple_gate, norm_final):
    depth = norm_mix.shape[0]
    assert depth == 1, "single-layer step"
    bp, s_len, d = x_prompt.shape
    bs, t_len, _ = x_sample.shape
    rw = w0.shape[1]
    n_rw_heads = rw // HEAD_DIM
    rw_cols = mu.shape[1]
    at_width = (w_in.shape[2] - rw_cols) // 3
    n_at_heads = at_width // HEAD_DIM
    win = cache_k_win.shape[2]
    assert s_len % (max(DILATIONS) * ATT_BLOCK) == 0 and s_len >= win and t_len == 8 and bs % 8 == 0
    assert all(wd // dl == ATT_BLOCK for wd, dl in zip(WINDOWS, DILATIONS)) and win >= max(WINDOWS)

    prm = _layer_params(0, norm_mix, w_in, mu, w0, w2, a0, a2, g2, k_k, k_a, r_k, ln_x_w, ln_x_b, attn_gain, w_out,
                        norm_ffn, w_group, b_group, w_expert_router, b_expert_router, w_gate, w_up, w_down,
                        norm_ple, w_ple, w_ple_gate, norm_final)

    n_p = bp * s_len
    xp = x_prompt.reshape(n_p, d)
    pr, q, k, v = _in_proj(xp, prm["norm_mix"], prm["w_in"], rw_cols, at_width, _tile(n_p, 256), True)
    tm = _tile(s_len, 512)
    r_, lw, km, vv, kk, kka, g, bonus = _rwkv_prep(pr, pr, prm, rw, tm, 1, s_len // tm)
    ct = _tile(s_len, 256)
    y, wkv_p = _wkv_prompt(bp, n_rw_heads, (r_, lw, km, vv, kk, kka), ct)
    attn = _attn_prompt(q, k, v, bp, n_at_heads)
    y_prompt = _ffn_and_tail(xp, y, g, bonus, attn, p_prompt[0].reshape(n_p, -1), prm, rw).reshape(bp, s_len, d)
    shift_p = pr.reshape(bp, s_len, rw_cols)[:, -1]
    keep = min(win, s_len)

    def window(slabs):
        z = slabs.reshape(-1, bp, s_len, 2, HEAD_DIM)[:, :, s_len - keep:]
        return jnp.transpose(z, (1, 2, 0, 3, 4)).reshape(bp, keep, n_at_heads, HEAD_DIM)

    kwin_p, vwin_p = window(k), window(v)

    n_s = bs * t_len
    xs = jnp.swapaxes(x_sample, 0, 1).reshape(n_s, d)
    ps = jnp.swapaxes(p_sample[0], 0, 1).reshape(n_s, -1)
    pr_s, q_s, k_s, v_s = _in_proj(xs, prm["norm_mix"], prm["w_in"], rw_cols, at_width, _tile(n_s, 256), False)
    r_, lw, km, vv, kk, kka, g, bonus = _rwkv_prep(pr_s, state_shift[0], prm, rw, n_s, bs, 1)
    y, wkv_s = _wkv_sample(t_len, bs, n_rw_heads, (r_, lw, km, vv, kk, kka), state_wkv[0], 4)
    to_t = lambda c: jnp.transpose(c[0], (0, 2, 3, 1)).reshape(bs, at_width, win)
    from_t = lambda c: jnp.transpose(c.reshape(bs, n_at_heads, HEAD_DIM, win), (0, 3, 1, 2))[None]
    at_s, kwin_s, vwin_s = _attn_sample(q_s, k_s, v_s, to_t(cache_k_win), to_t(cache_v_win), t_len, bs, n_at_heads)
    y_s = _ffn_and_tail(xs, y, g, bonus, at_s, ps, prm, rw)
    y_sample = jnp.swapaxes(y_s.reshape(t_len, bs, d), 0, 1)
    shift_s = pr_s[(t_len - 1) * bs:]

    return (y_prompt, y_sample, wkv_p[None], shift_p[None], kwin_p[None], vwin_p[None],
            wkv_s[None], shift_s[None], from_t(kwin_s), from_t(vwin_s))
```

```python
import functools

import numpy as np
import jax
import jax.numpy as jnp
from jax import lax
from jax.experimental import pallas as pl
from jax.experimental.pallas import tpu as pltpu

F32 = jnp.float32
BF16 = jnp.bfloat16

LANES = 128
HEAD_DIM = 64
PAIR = 2 * HEAD_DIM
DECAY_LORA = 32
AAA_LORA = 32
GATE_LORA = 64
GN_EPS = 64e-5
RMS_EPS = 1e-6
WINDOWS = (128, 512, 2048)
DILATIONS = (1, 4, 16)
ATT_BLOCK = 128
N_GROUPS = 4
EXPERTS_PER_GROUP = 8
N_EXPERTS = N_GROUPS * EXPERTS_PER_GROUP
ROUTER_LANES = 128
EXPERT_LANE0 = N_GROUPS
WKV_CHUNK = 64
VMEM_LIMIT = 56 * 1024 * 1024


def _cparams(n_axes):
    return pltpu.CompilerParams(dimension_semantics=("arbitrary",) * n_axes, vmem_limit_bytes=VMEM_LIMIT)


def _full(shape):
    nd = len(shape)
    return pl.BlockSpec(shape, lambda *_: (0,) * nd)


def _bdot(a, b):
    return jnp.dot(a.astype(BF16), b.astype(BF16), preferred_element_type=F32)


def _split2(a):
    hi = a.astype(BF16)
    lo = (a - hi.astype(F32)).astype(BF16)
    return hi, lo


def _hilo_dot(a, b_bf16):
    hi, lo = _split2(a)
    return (jnp.dot(hi, b_bf16, preferred_element_type=F32)
            + jnp.dot(lo, b_bf16, preferred_element_type=F32))


def _rms(x, g):
    return x * lax.rsqrt(jnp.mean(x * x, axis=-1, keepdims=True) + RMS_EPS) * g


def _in_proj_kernel(x_ref, g_ref, w_ref, pr_ref, q_ref, k_ref, v_ref, *, rw_cols, at_width, slabs):
    h = _rms(x_ref[...], g_ref[...]).astype(BF16)
    pr_ref[...] = jnp.dot(h, w_ref[:, :rw_cols], preferred_element_type=F32)
    for i, o_ref in enumerate((q_ref, k_ref, v_ref)):
        lo = rw_cols + i * at_width
        z = jnp.dot(h, w_ref[:, lo:lo + at_width], preferred_element_type=F32)
        if slabs:
            for s in range(at_width // PAIR):
                o_ref[s] = z[:, s * PAIR:(s + 1) * PAIR]
        else:
            o_ref[...] = z


def _in_proj(x, g, w_bf16, rw_cols, at_width, tm, slabs):
    n, d = x.shape
    row = lambda w: pl.BlockSpec((tm, w), lambda i: (i, 0))
    if slabs:
        qkv_spec = pl.BlockSpec((at_width // PAIR, tm, PAIR), lambda i: (0, i, 0))
        qkv_shape = jax.ShapeDtypeStruct((at_width // PAIR, n, PAIR), F32)
    else:
        qkv_spec, qkv_shape = row(at_width), jax.ShapeDtypeStruct((n, at_width), F32)
    return pl.pallas_call(
        functools.partial(_in_proj_kernel, rw_cols=rw_cols, at_width=at_width, slabs=slabs),
        grid=(n // tm,),
        in_specs=[row(d), _full(g.shape), _full(w_bf16.shape)],
        out_specs=[row(rw_cols)] + [qkv_spec] * 3,
        out_shape=[jax.ShapeDtypeStruct((n, rw_cols), F32)] + [qkv_shape] * 3,
        compiler_params=_cparams(1),
        name="in_proj",
    )(x, g, w_bf16)


def _win_proj_kernel(x_ref, g_ref, wt_ref, o_ref):
    h = _rms(x_ref[0], g_ref[...]).astype(BF16)
    o_ref[0] = lax.dot_general(wt_ref[...], h, (((1,), (1,)), ((), ())), preferred_element_type=F32)


def _win_proj(x, g, wt_bf16, keep, tm):
    seqs, s_len, d = x.shape
    cols = wt_bf16.shape[0]
    first = (s_len - keep) // tm
    return pl.pallas_call(
        _win_proj_kernel,
        grid=(seqs, keep // tm),
        in_specs=[pl.BlockSpec((1, tm, d), lambda b, j: (b, first + j, 0)), _full(g.shape), _full(wt_bf16.shape)],
        out_specs=pl.BlockSpec((1, cols, tm), lambda b, j: (b, 0, j)),
        out_shape=jax.ShapeDtypeStruct((seqs, cols, keep), F32),
        compiler_params=_cparams(2),
        name="win_proj",
    )(x, g, wt_bf16)


def _rwkv_prep_kernel(pr_ref, bnd_ref, mu_ref, w0_ref, w2_ref, a0_ref, a2_ref, g2_ref, kk_ref, ka_ref, rk_ref,
                      bd_ref, r_o, lw_o, k_o, v_o, kk_o, kka_o, g_o, bonus_o, *, rw, shift_rows, tiles_per_seq):
    pr = pr_ref[...]
    tm = pr.shape[0]
    if shift_rows == 1:
        first = pl.program_id(0) % tiles_per_seq == 0
        prev_row = jnp.where(first, 0.0, bnd_ref[7:8, :])
        row = lax.broadcasted_iota(jnp.int32, pr.shape, 0)
        prev = jnp.where(row == 0, prev_row, pltpu.roll(pr, 1, 0))
    else:
        prev = jnp.concatenate([bnd_ref[...], pr[:tm - shift_rows]], axis=0)
    xm = pr + (prev - pr) * mu_ref[...]
    xr, xk, xv, xl = xm[:, :rw], xm[:, rw:2 * rw], xm[:, 2 * rw:3 * rw], xm[:, 3 * rw:]
    bd = bd_ref[...]
    z = -(w0_ref[...] + _bdot(jnp.tanh(xl), w2_ref[...]))
    softplus = jnp.maximum(z, 0.0) + jnp.log(1.0 + jnp.exp(-jnp.abs(z)))
    lw = -jnp.exp(-softplus - 0.5)
    a = jax.nn.sigmoid(a0_ref[...] + _bdot(xl, a2_ref[...]))
    g = _bdot(jax.nn.sigmoid(xl), g2_ref[...])
    kk = xk * kk_ref[...]
    kk = kk / jnp.maximum(jnp.sqrt(_hilo_dot(kk * kk, bd)), 1e-12)
    kmod = xk * (1.0 + (a - 1.0) * ka_ref[...])
    bonus = _hilo_dot(xr * kmod * rk_ref[...], bd) * xv
    r_o[...] = xr
    lw_o[...] = lw
    k_o[...] = kmod
    v_o[...] = xv
    kk_o[...] = kk
    kka_o[...] = kk * a
    g_o[...] = g
    bonus_o[...] = bonus


def _rwkv_prep(pr, bnd, prm, rw, tm, shift_rows, tiles_per_seq):
    n, c = pr.shape
    row = lambda w: pl.BlockSpec((tm, w), lambda i: (i, 0))
    if shift_rows == 1:
        bnd_spec = pl.BlockSpec((8, c), lambda i: (jnp.maximum(i * (tm // 8) - 1, 0), 0))
    else:
        bnd_spec = _full(bnd.shape)
    params = [prm[k] for k in ("mu", "w0", "w2p", "a0", "a2p", "g2p", "k_k", "k_a", "r_k", "bd")]
    return pl.pallas_call(
        functools.partial(_rwkv_prep_kernel, rw=rw, shift_rows=shift_rows, tiles_per_seq=tiles_per_seq),
        grid=(n // tm,),
        in_specs=[row(c), bnd_spec] + [_full(p.shape) for p in params],
        out_specs=[row(rw)] * 8,
        out_shape=[jax.ShapeDtypeStruct((n, rw), F32)] * 8,
        compiler_params=_cparams(1),
        name="rwkv_prep",
    )(pr, bnd, *params)


def _cumsum_rows(x):
    n = x.shape[0]
    row = lax.broadcasted_iota(jnp.int32, x.shape, 0)
    s = 1
    while s < n:
        x = x + jnp.where(row >= s, pltpu.roll(x, s, 0), 0.0)
        s *= 2
    return x


def _lockstep(gens):
    results = [None] * len(gens)
    live = list(range(len(gens)))
    while live:
        still = []
        for i in live:
            try:
                next(gens[i])
                still.append(i)
            except StopIteration as stop:
                results[i] = stop.value
        live = still
    return results


def _wkv_head_chunk(pt, rt, kt, qt, kt_end, qt_end, vh, p_end, s_h, masks):
    strict, incl, eye = masks
    c = pt.shape[0]
    nt = lambda a_, b_: lax.dot_general(a_, b_, (((1,), (1,)), ((), ())), preferred_element_type=F32)
    nn = lambda a_, b_: jnp.dot(a_, b_, preferred_element_type=F32)
    tn = lambda a_, b_: lax.dot_general(a_, b_, (((0,), (0,)), ((), ())), preferred_element_type=F32)
    lhs = jnp.concatenate([pt, rt], axis=0)
    gq = nt(lhs, qt)
    gk = nt(lhs, kt)
    yield
    l_qp = jnp.where(strict, gq[:c], 0.0)
    l_kp = jnp.where(strict, gk[:c], 0.0)
    a_rq = jnp.where(incl, gq[c:], 0.0)
    a_rk = jnp.where(incl, gk[c:], 0.0)
    t_inv = eye - l_qp
    pw = -l_qp
    lv_av = nn(jnp.concatenate([l_kp, a_rk], axis=0), vh)
    for _ in range(int(np.log2(c)) - 1):
        pw = nn(pw, pw)
        yield
        t_new = nn(pw, t_inv)
        yield
        t_inv = t_inv + t_new
    w_mat = nn(t_inv, pt)
    u_v = nn(t_inv, lv_av[:c])
    yield
    ws = nt(jnp.concatenate([w_mat, rt], axis=0), s_h)
    yield
    u = u_v + ws[:c]
    au = nn(a_rq, u)
    upd = tn(jnp.concatenate([vh, -u], axis=0), jnp.concatenate([kt_end, qt_end], axis=0))
    yield
    return ws[c:] + lv_av[c:] - au, s_h * p_end + upd


def _pair_stack(z, first):
    return jnp.concatenate([jnp.where(first, z, 0.0), jnp.where(first, 0.0, z)], axis=0).astype(BF16)


def _wkv_pair_static(slabs, masks):
    r, lw, k, v, kk, kka = slabs
    first, strict, incl, eye = masks
    c = r.shape[0]
    n2 = 2 * c
    nt = lambda a_, b_: lax.dot_general(a_, b_, (((1,), (1,)), ((), ())), preferred_element_type=F32)
    nn = lambda a_, b_: jnp.dot(a_, b_, preferred_element_type=F32)
    cum = _cumsum_rows(lw)
    cum_last = cum[c - 1:c, :]
    d_end = jnp.exp(cum_last - cum)
    p_inv = jnp.exp(-cum)
    ps = _pair_stack(kk * jnp.exp(cum - lw), first)
    rs = _pair_stack(r * jnp.exp(cum), first)
    qs = _pair_stack(kka * p_inv, first)
    ks = _pair_stack(k * p_inv, first)
    vs = _pair_stack(v, first)
    ends = jnp.concatenate([_pair_stack(k * d_end, first), _pair_stack(kka * d_end, first)], axis=0)
    g = nt(jnp.concatenate([ps, rs], axis=0), jnp.concatenate([qs, ks], axis=0))
    yield
    l_qp = jnp.where(strict, g[:n2, :n2], 0.0)
    l_kp = jnp.where(strict, g[:n2, n2:], 0.0).astype(BF16)
    a_rq = jnp.where(incl, g[n2:, :n2], 0.0).astype(BF16)
    a_rk = jnp.where(incl, g[n2:, n2:], 0.0).astype(BF16)
    t_inv = eye - l_qp
    pw = (-l_qp).astype(BF16)
    lv_av = nn(jnp.concatenate([l_kp, a_rk], axis=0), vs)
    pw2 = nn(pw, pw)
    yield
    lv, av = lv_av[:n2].astype(BF16), lv_av[n2:]
    pw = pw2.astype(BF16)
    n_it = int(np.log2(c)) - 1
    for i in range(n_it):
        if i < n_it - 1:
            out = nn(pw, jnp.concatenate([pw, t_inv.astype(BF16)], axis=1))
            yield
            pw = out[:, :n2].astype(BF16)
            t_inv = t_inv + out[:, n2:]
        else:
            out = nn(pw, t_inv.astype(BF16))
            yield
            t_inv = t_inv + out
    tp = nn(t_inv.astype(BF16), jnp.concatenate([ps, lv], axis=1))
    yield
    return dict(w=tp[:, :PAIR].astype(BF16), u_v=tp[:, PAIR:], rs=rs, vs=vs, a_rq=a_rq, av=av, ends=ends,
                p_end=jnp.exp(cum_last))


def _wkv_pair_step(st, s_bd):
    nt = lambda a_, b_: lax.dot_general(a_, b_, (((1,), (1,)), ((), ())), preferred_element_type=F32)
    tn = lambda a_, b_: lax.dot_general(a_, b_, (((0,), (0,)), ((), ())), preferred_element_type=F32)
    n2 = st["w"].shape[0]
    ws = nt(jnp.concatenate([st["w"], st["rs"]], axis=0), s_bd.astype(BF16))
    yield
    u = st["u_v"] + ws[:n2]
    upd = tn(jnp.concatenate([st["vs"], (-u).astype(BF16)], axis=0), st["ends"])
    au = jnp.dot(st["a_rq"], u.astype(BF16), preferred_element_type=F32)
    yield
    y_bd = ws[n2:] + st["av"] - au
    return y_bd[:n2 // 2] + y_bd[n2 // 2:], s_bd * st["p_end"] + upd


def _wkv_prompt_kernel(r_ref, lw_ref, k_ref, v_ref, kk_ref, kka_ref, y_ref, sT_ref, s_scr, *, chunks, tiles_per_seq):
    j = pl.program_id(0) % tiles_per_seq
    n_pairs = s_scr.shape[0]
    c = WKV_CHUNK

    @pl.when(j == 0)
    def _():
        s_scr[...] = jnp.zeros_like(s_scr)

    lane = lax.broadcasted_iota(jnp.int32, (c, PAIR), 1)
    ri = lax.broadcasted_iota(jnp.int32, (2 * c, 2 * c), 0)
    ci = lax.broadcasted_iota(jnp.int32, (2 * c, 2 * c), 1)
    rt_, ct_ = ri % c, ci % c
    masks = (lane < HEAD_DIM, rt_ > ct_, rt_ >= ct_, (ri == ci).astype(F32))

    gens = []
    for ch in range(chunks):
        rows = slice(ch * c, (ch + 1) * c)
        for p in range(n_pairs):
            lanes = slice(p * PAIR, (p + 1) * PAIR)
            slabs = [ref[rows, lanes] for ref in (r_ref, lw_ref, k_ref, v_ref, kk_ref, kka_ref)]
            gens.append(_wkv_pair_static(slabs, masks))
    static = _lockstep(gens)
    states = [s_scr[p] for p in range(n_pairs)]
    for ch in range(chunks):
        outs = _lockstep([_wkv_pair_step(static[ch * n_pairs + p], states[p]) for p in range(n_pairs)])
        for p, (y, s_new) in enumerate(outs):
            states[p] = s_new
            y_ref[ch * c:(ch + 1) * c, p * PAIR:(p + 1) * PAIR] = y
    for p in range(n_pairs):
        s_scr[p] = states[p]

    @pl.when(j == tiles_per_seq - 1)
    def _():
        for p in range(n_pairs):
            sT_ref[0, 2 * p] = s_scr[p, :HEAD_DIM, :HEAD_DIM]
            sT_ref[0, 2 * p + 1] = s_scr[p, HEAD_DIM:, HEAD_DIM:]


def _wkv_prompt(seqs, n_heads, arrays, ct):
    n, w = arrays[0].shape
    tiles_per_seq = n // seqs // ct
    row = pl.BlockSpec((ct, w), lambda i: (i, 0))
    st_shape = (seqs, n_heads, HEAD_DIM, HEAD_DIM)
    return pl.pallas_call(
        functools.partial(_wkv_prompt_kernel, chunks=ct // WKV_CHUNK, tiles_per_seq=tiles_per_seq),
        grid=(n // ct,),
        in_specs=[row] * 6,
        out_specs=[row, pl.BlockSpec((1,) + st_shape[1:], lambda i: (i // tiles_per_seq, 0, 0, 0))],
        out_shape=[jax.ShapeDtypeStruct((n, w), F32), jax.ShapeDtypeStruct(st_shape, F32)],
        scratch_shapes=[pltpu.VMEM((n_heads // 2, PAIR, PAIR), F32)],
        compiler_params=_cparams(1),
        name="wkv_prompt",
    )(*arrays)


def _wkv_sample_kernel(r_ref, lw_ref, k_ref, v_ref, kk_ref, kka_ref, s0_ref, y_ref, sT_ref, *, seqs_per_step):
    n_heads = s0_ref.shape[1]
    c = r_ref.shape[0]
    ri = lax.broadcasted_iota(jnp.int32, (c, c), 0)
    ci = lax.broadcasted_iota(jnp.int32, (c, c), 1)
    masks = (ri > ci, ri >= ci, (ri == ci).astype(F32))
    gens = []
    for s in range(seqs_per_step):
        r, lw, k, v, kk, kka = [ref[:, s, 0, :] for ref in (r_ref, lw_ref, k_ref, v_ref, kk_ref, kka_ref)]
        cum = _cumsum_rows(lw)
        cum_last = cum[c - 1:c, :]
        p_inv = jnp.exp(-cum)
        d_end = jnp.exp(cum_last - cum)
        ops = (kk * jnp.exp(cum - lw), r * jnp.exp(cum), k * p_inv, kka * p_inv, k * d_end, kka * d_end, v,
               jnp.exp(cum_last))
        for h in range(n_heads):
            hs = slice(h * HEAD_DIM, (h + 1) * HEAD_DIM)
            gens.append(_wkv_head_chunk(*[z[:, hs] for z in ops], s0_ref[s, h], masks))
    outs = _lockstep(gens)
    for s in range(seqs_per_step):
        heads = outs[s * n_heads:(s + 1) * n_heads]
        y_ref[:, s, 0, :] = jnp.concatenate([y for y, _ in heads], axis=1)
        for h, (_, s_new) in enumerate(heads):
            sT_ref[s, h] = s_new


def _wkv_sample(t_len, seqs, n_heads, arrays, s0, seqs_per_step):
    w = arrays[0].shape[-1]
    arrays = [a.reshape(t_len, seqs, 1, w) for a in arrays]
    tok = pl.BlockSpec((t_len, seqs_per_step, 1, w), lambda i: (0, i, 0, 0))
    st = pl.BlockSpec((seqs_per_step, n_heads, HEAD_DIM, HEAD_DIM), lambda i: (i, 0, 0, 0))
    y, s_t = pl.pallas_call(
        functools.partial(_wkv_sample_kernel, seqs_per_step=seqs_per_step),
        grid=(seqs // seqs_per_step,),
        in_specs=[tok] * 6 + [st],
        out_specs=[tok, st],
        out_shape=[jax.ShapeDtypeStruct((t_len, seqs, 1, w), F32), jax.ShapeDtypeStruct(s0.shape, F32)],
        compiler_params=_cparams(1),
        name="wkv_sample",
    )(*arrays, s0)
    return y.reshape(t_len * seqs, w), s_t


ATT_SPAN = max(DILATIONS) * ATT_BLOCK


def _attn_prompt_kernel(q_ref, kp_ref, kc_ref, vp_ref, vc_ref, o_ref, kx, vx, ob, lb, bias_scr, *, n_heads,
                        tiles_per_iter):
    slab, blk = pl.program_id(1), pl.program_id(2)
    span = ATT_SPAN
    kx[:span] = kp_ref[0]
    kx[span:] = kc_ref[0]
    vx[:span] = vp_ref[0]
    vx[span:] = vc_ref[0]
    qi = lax.broadcasted_iota(jnp.int32, (ATT_BLOCK, 2 * ATT_BLOCK), 0)
    ki = lax.broadcasted_iota(jnp.int32, (ATT_BLOCK, 2 * ATT_BLOCK), 1)
    steps = ATT_BLOCK + qi - ki
    band = (steps >= 0) & (steps <= ATT_BLOCK)
    stepsf = steps.astype(F32)
    first = lax.broadcasted_iota(jnp.int32, (ATT_BLOCK, PAIR), 1) < HEAD_DIM
    log2e = float(np.log2(np.e))
    scale = HEAD_DIM ** -0.5 * log2e
    for h2 in range(2):
        alibi = jnp.exp2(jnp.zeros_like(stepsf) - (8.0 / n_heads) * (2 * slab + 1 + h2).astype(F32)) * (stepsf * log2e)
        for br, dil in enumerate(DILATIONS):
            bias_scr[0, br, h2] = jnp.where(band, alibi * float(dil), jnp.inf)
            bias_scr[1, br, h2] = jnp.where(band & (ki >= ATT_BLOCK), alibi * float(dil), jnp.inf)

    def head(q_h, k_t, v_t, bias):
        s = lax.dot_general(q_h, k_t, (((1,), (1,)), ((), ())), preferred_element_type=F32)
        yield
        s = s - bias
        m = jnp.max(s, axis=-1, keepdims=True)
        e = jnp.exp2(s - m)
        den = jnp.sum(e, axis=-1, keepdims=True)
        o = jnp.dot(e.astype(BF16), v_t, preferred_element_type=F32)
        yield
        return o / den, m + jnp.log2(den)

    def body(it, carry):
        gens, where_to = [], []
        for u in range(tiles_per_iter):
            t = it * tiles_per_iter + u
            for br, dil in enumerate(DILATIONS):
                sub = t // dil
                base = t % dil + dil * ATT_BLOCK * sub
                rows_q = pl.ds(base, ATT_BLOCK, stride=dil)
                rows_kv = pl.ds(span + base - dil * ATT_BLOCK, 2 * ATT_BLOCK, stride=dil)
                q_t = q_ref[0, rows_q, :] * scale
                k_t = kx[rows_kv, :].astype(BF16)
                v_t = vx[rows_kv, :].astype(BF16)
                at_start = jnp.where((blk > 0) | (sub > 0), 0, 1)
                for h2 in range(2):
                    q_h = jnp.where(first if h2 == 0 else ~first, q_t, 0.0).astype(BF16)
                    gens.append(head(q_h, k_t, v_t, bias_scr[at_start, br, h2]))
                where_to.append((br, rows_q))
        res = _lockstep(gens)
        for i, (br, rows_q) in enumerate(where_to):
            (o0, l0), (o1, l1) = res[2 * i], res[2 * i + 1]
            ob[br, rows_q, :] = jnp.where(first, o0, o1)
            lb[br, rows_q, :] = jnp.where(first, l0, l1)
        return carry

    lax.fori_loop(0, span // ATT_BLOCK // tiles_per_iter, body, 0)
    ls = [lb[br] for br in range(len(DILATIONS))]
    m = functools.reduce(jnp.maximum, ls)
    es = [jnp.exp2(l - m) for l in ls]
    o_ref[0] = sum(e * ob[br] for br, e in enumerate(es)) / sum(es)


def _attn_prompt(q, k, v, seqs, n_heads):
    n_slabs, n, _ = q.shape
    n_blk = n // seqs // ATT_SPAN
    cur = pl.BlockSpec((1, ATT_SPAN, PAIR), lambda b, s, i: (s, b * n_blk + i, 0))
    prev = pl.BlockSpec((1, ATT_SPAN, PAIR), lambda b, s, i: (s, b * n_blk + jnp.maximum(i - 1, 0), 0))
    return pl.pallas_call(
        functools.partial(_attn_prompt_kernel, n_heads=n_heads, tiles_per_iter=2),
        grid=(seqs, n_slabs, n_blk),
        in_specs=[cur, prev, cur, prev, cur],
        out_specs=cur,
        out_shape=jax.ShapeDtypeStruct(q.shape, F32),
        scratch_shapes=[pltpu.VMEM((2 * ATT_SPAN, PAIR), F32)] * 2
                       + [pltpu.VMEM((len(DILATIONS), ATT_SPAN, PAIR), F32)] * 2
                       + [pltpu.VMEM((2, len(DILATIONS), 2, ATT_BLOCK, 2 * ATT_BLOCK), F32)],
        compiler_params=_cparams(3),
        name="attn_prompt",
    )(q, k, k, v, v)


def _attn_sample_kernel(q_ref, kn_ref, vn_ref, kc_ref, vc_ref, o_ref, ko_ref, vo_ref, *, n_heads):
    t_len = q_ref.shape[0]
    w, win = kc_ref.shape[1], kc_ref.shape[2]
    rows = n_heads * t_len
    q = q_ref[:, 0, 0, :]
    kn = kn_ref[:, 0, 0, :]
    vn = vn_ref[:, 0, 0, :]
    nt = lambda a_, b_: lax.dot_general(a_, b_, (((1,), (1,)), ((), ())), preferred_element_type=F32)
    tn = lambda a_, b_: lax.dot_general(a_, b_, (((0,), (0,)), ((), ())), preferred_element_type=F32)

    pad_rows = 16
    place = (lax.broadcasted_iota(jnp.int32, (pad_rows, LANES), 1)
             == lax.broadcasted_iota(jnp.int32, (pad_rows, LANES), 0) + (LANES - t_len)).astype(BF16)
    lane = lax.broadcasted_iota(jnp.int32, (LANES, LANES), 1)

    def transposed_tail(z):
        zp = jnp.concatenate([z, jnp.zeros((pad_rows - t_len, w), F32)], axis=0)
        hi = zp.astype(BF16)
        r1 = zp - hi.astype(F32)
        mid = r1.astype(BF16)
        lo = (r1 - mid.astype(F32)).astype(BF16)
        return tn(hi, place) + tn(mid, place) + tn(lo, place)

    for src, new, dst in ((kc_ref, kn, ko_ref), (vc_ref, vn, vo_ref)):
        tail = transposed_tail(new)
        for r0 in range(0, w, LANES):
            rolled = pltpu.roll(src[0, r0:r0 + LANES, :], win - t_len, 1)
            dst[0, r0:r0 + LANES, :win - LANES] = rolled[:, :win - LANES]
            dst[0, r0:r0 + LANES, win - LANES:] = jnp.where(lane < LANES - t_len, rolled[:, win - LANES:],
                                                             tail[r0:r0 + LANES])

    qrep = jnp.concatenate([q] * n_heads, axis=0)
    rh = lax.broadcasted_iota(jnp.int32, (rows, w), 0) // t_len
    ch = lax.broadcasted_iota(jnp.int32, (rows, w), 1) // HEAD_DIM
    qexp = jnp.where(rh == ch, qrep, 0.0).astype(BF16)
    scale = HEAD_DIM ** -0.5

    def weights(n_keys, key_pos0):
        ri = lax.broadcasted_iota(jnp.int32, (rows, n_keys), 0)
        ki = lax.broadcasted_iota(jnp.int32, (rows, n_keys), 1)
        dist = win + ri % t_len - (key_pos0 + ki)
        cnt = jnp.zeros((rows, n_keys), F32)
        for wdw, dil in zip(WINDOWS, DILATIONS):
            ok = (dist >= 0) & (dist % dil == 0) & (dist <= wdw)
            cnt = cnt + jnp.where(ok, 1.0, 0.0)
        slope = jnp.exp2(-8.0 * (ri // t_len + 1).astype(F32) / n_heads)
        return cnt, slope * dist.astype(F32)

    cnt_c, bias_c = weights(win, 0)
    cnt_n, bias_n = weights(t_len, win)
    s_c = jnp.where(cnt_c > 0, jnp.dot(qexp, kc_ref[0].astype(BF16), preferred_element_type=F32) * scale - bias_c,
                    -jnp.inf)
    s_n = jnp.where(cnt_n > 0, nt(qexp, kn.astype(BF16)) * scale - bias_n, -jnp.inf)
    m = jnp.maximum(jnp.max(s_c, axis=-1, keepdims=True), jnp.max(s_n, axis=-1, keepdims=True))
    e_c = cnt_c * jnp.exp(s_c - m)
    e_n = cnt_n * jnp.exp(s_n - m)
    den = jnp.sum(e_c, axis=-1, keepdims=True) + jnp.sum(e_n, axis=-1, keepdims=True)
    acc = (nt(e_c.astype(BF16), vc_ref[0].astype(BF16)) + jnp.dot(e_n, vn, preferred_element_type=F32)) / den
    acc = jnp.where(rh == ch, acc, 0.0)
    out = acc[:t_len]
    for h in range(1, n_heads):
        out = out + acc[h * t_len:(h + 1) * t_len]
    o_ref[:, 0, 0, :] = out


def _attn_sample(q, k, v, kc, vc, t_len, seqs, n_heads):
    _, w, win = kc.shape
    tok = pl.BlockSpec((t_len, 1, 1, w), lambda b: (0, b, 0, 0))
    cache = pl.BlockSpec((1, w, win), lambda b: (b, 0, 0))
    v4 = lambda a: a.reshape(t_len, seqs, 1, w)
    o, ko, vo = pl.pallas_call(
        functools.partial(_attn_sample_kernel, n_heads=n_heads),
        grid=(seqs,),
        in_specs=[tok, tok, tok, cache, cache],
        out_specs=[tok, cache, cache],
        out_shape=[jax.ShapeDtypeStruct((t_len, seqs, 1, w), F32)] + [jax.ShapeDtypeStruct(kc.shape, F32)] * 2,
        compiler_params=_cparams(1),
        name="attn_sample",
    )(v4(q), v4(k), v4(v), kc, vc)
    return o.reshape(t_len * seqs, w), ko, vo


def _post_kernel(x_ref, y_ref, g_ref, bonus_ref, at_ref, lnw_ref, lnb_ref, gain_ref, wout_ref, nffn_ref, wr_hi_ref,
                 wr_lo_ref, br_ref, bd_ref, x1_ref, hn_ref, comb_ref, sel_ref, *, attn_slabs, rw):

    bd = bd_ref[...]
    y = y_ref[...]
    mean = _hilo_dot(y, bd) * (1.0 / HEAD_DIM)
    d = y - mean
    var = _hilo_dot(d * d, bd) * (1.0 / HEAD_DIM)
    yn = d * lax.rsqrt(var + GN_EPS) * lnw_ref[...] + lnb_ref[...]
    rw_out = (yn + bonus_ref[...]) * g_ref[...]

    if attn_slabs:
        at = jnp.concatenate([at_ref[s] for s in range(at_ref.shape[0])], axis=1)
    else:
        at = at_ref[...]
    at_out = _rms(at, gain_ref[...])

    x1 = (x_ref[...] + jnp.dot(rw_out.astype(BF16), wout_ref[:rw, :], preferred_element_type=F32)
          + jnp.dot(at_out.astype(BF16), wout_ref[rw:, :], preferred_element_type=F32))
    x1_ref[...] = x1
    hn = _rms(x1, nffn_ref[...])
    hn_ref[...] = hn.astype(BF16)

    hi, lo = _split2(hn)
    logits = (jnp.dot(hi, wr_hi_ref[...], preferred_element_type=F32)
              + jnp.dot(lo, wr_hi_ref[...], preferred_element_type=F32)
              + jnp.dot(hi, wr_lo_ref[...], preferred_element_type=F32)) + br_ref[...]
    lane = lax.broadcasted_iota(jnp.int32, logits.shape, 1)
    big = jnp.int32(ROUTER_LANES)
    first_max = lambda z: jnp.min(jnp.where(z == jnp.max(z, axis=-1, keepdims=True), lane, big), axis=-1, keepdims=True)
    gl = jnp.where(lane < N_GROUPS, logits, -jnp.inf)
    gsel = first_max(gl)
    gw = 1.0 / jnp.sum(jnp.exp(gl - jnp.max(gl, axis=-1, keepdims=True)), axis=-1, keepdims=True)
    e_lane = lane - EXPERT_LANE0
    in_group = (e_lane >= 0) & (e_lane < N_EXPERTS) & (e_lane // EXPERTS_PER_GROUP == gsel)
    el = jnp.where(in_group, logits, -jnp.inf)
    m1 = jnp.max(el, axis=-1, keepdims=True)
    i1 = first_max(el)
    el2 = jnp.where(lane == i1, -jnp.inf, el)
    m2 = jnp.max(el2, axis=-1, keepdims=True)
    i2 = first_max(el2)
    e2 = jnp.exp(m2 - m1)
    w1 = gw / (1.0 + e2)
    comb_ref[...] = jnp.where(lane == i1, w1, 0.0) + jnp.where(lane == i2, w1 * e2, 0.0)
    sel_ref[...] = jnp.where((lane == i1) | (lane == i2), 1.0, 0.0)


def _post(x, y, g, bonus, attn, prm, rw, tm):
    n, d = x.shape
    row = lambda w: pl.BlockSpec((tm, w), lambda i: (i, 0))
    attn_slabs = attn.ndim == 3
    at_spec = pl.BlockSpec((attn.shape[0], tm, PAIR), lambda i: (0, i, 0)) if attn_slabs else row(attn.shape[1])
    params = [prm[k] for k in ("ln_x_w", "ln_x_b", "attn_gain", "w_out", "norm_ffn", "wr_hi", "wr_lo", "b_router", "bd")]
    return pl.pallas_call(
        functools.partial(_post_kernel, attn_slabs=attn_slabs, rw=rw),
        grid=(n // tm,),
        in_specs=[row(d), row(rw), row(rw), row(rw), at_spec] + [_full(p.shape) for p in params],
        out_specs=[row(d), row(d), row(ROUTER_LANES), row(ROUTER_LANES)],
        out_shape=[jax.ShapeDtypeStruct((n, d), F32), jax.ShapeDtypeStruct((n, d), BF16)]
                  + [jax.ShapeDtypeStruct((n, ROUTER_LANES), F32)] * 2,
        compiler_params=_cparams(1),
        name="post",
    )(x, y, g, bonus, attn, *params)


MOE_ROW_BLOCK = 128
MOE_EXPERTS_PER_STEP = 2
MOE_SEG_ALIGN = 16
TOP_K = 2


def _moe_rows(tm):
    rows = TOP_K * tm + N_EXPERTS * (MOE_SEG_ALIGN - 1)
    return -(-rows // LANES) * LANES


def _moe_kernel(hn_ref, comb_ref, sel_ref, tri_ref, wg_ref, wu_ref, wd_ref, o_ref, p_all, xs, ys, cs, meta):
    e = pl.program_id(1)
    tm = hn_ref.shape[0]
    rows = p_all.shape[1]
    tn = lambda a_, b_: lax.dot_general(a_, b_, (((0,), (0,)), ((), ())), preferred_element_type=F32)

    @pl.when(e == 0)
    def _():
        sel = sel_ref[...]
        picked = sel > 0.0
        rank = jnp.dot(tri_ref[...], sel.astype(BF16), preferred_element_type=F32)
        cnt = rank[tm - 1:tm] + sel[tm - 1:tm]
        seg = jnp.ceil(cnt * (1.0 / MOE_SEG_ALIGN))
        li = lax.broadcasted_iota(jnp.int32, (LANES, LANES), 0)
        lj = lax.broadcasted_iota(jnp.int32, (LANES, LANES), 1)
        before = (li < lj).astype(BF16)
        off = jnp.dot(jnp.broadcast_to(seg, (8, LANES)).astype(BF16), before,
                      preferred_element_type=F32)[0:1] * float(MOE_SEG_ALIGN)
        meta[0:1, :] = cnt
        meta[1:2, :] = off
        dest = off + rank
        d_lo = jnp.min(jnp.where(picked, dest, 1e9), axis=-1, keepdims=True)
        d_hi = jnp.max(jnp.where(picked, dest, -1.0), axis=-1, keepdims=True)
        comb = comb_ref[...]
        w_lo = jnp.sum(jnp.where(picked & (dest == d_lo), comb, 0.0), axis=-1, keepdims=True)
        w_hi = jnp.sum(jnp.where(picked & (dest == d_hi), comb, 0.0), axis=-1, keepdims=True)
        d_lo, d_hi = d_lo.astype(jnp.int32), d_hi.astype(jnp.int32)
        hn = hn_ref[...]
        chunk = next(c for c in (256, 128) if rows % c == 0)
        for c0 in range(0, rows, chunk):
            row_id = c0 + lax.broadcasted_iota(jnp.int32, (tm, chunk), 1)
            is_lo = row_id == d_lo
            is_hi = row_id == d_hi
            one = jnp.where(is_lo | is_hi, 1.0, 0.0).astype(BF16)
            p_all[:, c0:c0 + chunk] = one
            xs[c0:c0 + chunk, :] = tn(one, hn).astype(BF16)
            crow = jnp.sum(jnp.where(is_lo, w_lo, 0.0) + jnp.where(is_hi, w_hi, 0.0), axis=0, keepdims=True)
            for k0 in range(0, chunk, LANES):
                diag = jnp.where(li == lj, jnp.broadcast_to(crow[:, k0:k0 + LANES], (LANES, LANES)), 0.0)
                cs[c0 + k0:c0 + k0 + LANES, :] = jnp.broadcast_to(jnp.sum(diag, axis=1, keepdims=True), (LANES, LANES))
        xs[rows:, :] = jnp.zeros((xs.shape[0] - rows, xs.shape[1]), BF16)
        ys[...] = jnp.zeros_like(ys)
        cs[rows:, :] = jnp.zeros((cs.shape[0] - rows, LANES), F32)

    lane = lax.broadcasted_iota(jnp.int32, (1, LANES), 1)
    for j in range(wg_ref.shape[0]):
        mine = lane == e * wg_ref.shape[0] + j + EXPERT_LANE0
        n_e = jnp.sum(jnp.where(mine, meta[0:1, :], 0.0)).astype(jnp.int32)
        off_e = jnp.sum(jnp.where(mine, meta[1:2, :], 0.0)).astype(jnp.int32)

        def block(b, carry, j=j, off_e=off_e):
            r = pl.ds(pl.multiple_of(off_e + b * MOE_ROW_BLOCK, MOE_SEG_ALIGN), MOE_ROW_BLOCK)
            x = xs[r, :]
            gate = jnp.dot(x, wg_ref[j], preferred_element_type=F32)
            up = jnp.dot(x, wu_ref[j], preferred_element_type=F32)
            act = (gate * jax.nn.sigmoid(gate) * up * cs[r, 0:1]).astype(BF16)
            ys[r, :] = jnp.dot(act, wd_ref[j], preferred_element_type=F32).astype(BF16)
            return carry

        lax.fori_loop(0, (n_e + MOE_ROW_BLOCK - 1) // MOE_ROW_BLOCK, block, 0)

    @pl.when(e == pl.num_programs(1) - 1)
    def _():
        o_ref[...] = jnp.dot(p_all[...], ys[:rows, :], preferred_element_type=F32)


def _moe(hn, comb, sel, wg, wu, wd, tm):
    n, d = hn.shape
    n_exp, _, de = wg.shape
    rows = _moe_rows(tm)
    tri = jnp.asarray(np.tril(np.ones((tm, tm), np.float32), -1), BF16)
    tile = lambda w: pl.BlockSpec((tm, w), lambda i, e: (i, 0))
    return pl.pallas_call(
        _moe_kernel,
        grid=(n // tm, n_exp // MOE_EXPERTS_PER_STEP),
        in_specs=[tile(d), tile(ROUTER_LANES), tile(ROUTER_LANES),
                  pl.BlockSpec((tm, tm), lambda i, e: (0, 0)),
                  pl.BlockSpec((MOE_EXPERTS_PER_STEP, d, de), lambda i, e: (e, 0, 0)),
                  pl.BlockSpec((MOE_EXPERTS_PER_STEP, d, de), lambda i, e: (e, 0, 0)),
                  pl.BlockSpec((MOE_EXPERTS_PER_STEP, de, d), lambda i, e: (e, 0, 0))],
        out_specs=tile(d),
        out_shape=jax.ShapeDtypeStruct((n, d), F32),
        scratch_shapes=[pltpu.VMEM((tm, rows), BF16),
                        pltpu.VMEM((rows + MOE_ROW_BLOCK, d), BF16), pltpu.VMEM((rows + MOE_ROW_BLOCK, d), BF16),
                        pltpu.VMEM((rows + MOE_ROW_BLOCK, LANES), F32), pltpu.VMEM((8, LANES), F32)],
        compiler_params=_cparams(2),
        name="moe",
    )(hn, comb, sel, tri, wg, wu, wd)


def _tail_kernel(x1_ref, moe_ref, p_ref, nple_ref, wgate_ref, wple_ref, nfin_ref, y_ref):
    x2 = x1_ref[...] + moe_ref[...]
    gate = jax.nn.sigmoid(jnp.dot(_rms(x2, nple_ref[...]).astype(BF16), wgate_ref[...], preferred_element_type=F32))
    x3 = x2 + jnp.dot(p_ref[...].astype(BF16), wple_ref[...], preferred_element_type=F32) * gate
    y_ref[...] = _rms(x3, nfin_ref[...])


def _tail(x1, moe, p, prm, tm):
    n, d = x1.shape
    row = lambda w: pl.BlockSpec((tm, w), lambda i: (i, 0))
    params = [prm[k] for k in ("norm_ple", "w_ple_gate", "w_ple", "norm_final")]
    return pl.pallas_call(
        _tail_kernel,
        grid=(n // tm,),
        in_specs=[row(d), row(d), row(p.shape[1])] + [_full(a.shape) for a in params],
        out_specs=row(d),
        out_shape=jax.ShapeDtypeStruct((n, d), F32),
        compiler_params=_cparams(1),
        name="tail",
    )(x1, moe, p, *params)


def _layer_params(i, norm_mix, w_in, mu, w0, w2, a0, a2, g2, k_k, k_a, r_k, ln_x_w, ln_x_b, attn_gain, w_out,
                  norm_ffn, w_group, b_group, w_expert_router, b_expert_router, w_gate, w_up, w_down,
                  norm_ple, w_ple, w_ple_gate, norm_final):
    rw = w0.shape[1]
    d = w_in.shape[1]
    vec = lambda a: a[i].reshape(1, -1)
    lora_rows = DECAY_LORA + AAA_LORA + GATE_LORA

    def lora(w, lo):
        return jnp.zeros((lora_rows, rw), F32).at[lo:lo + w.shape[0]].set(w).astype(BF16)

    head = np.arange(rw) // HEAD_DIM
    w_router = jnp.zeros((d, ROUTER_LANES), F32)
    w_router = w_router.at[:, :N_GROUPS].set(w_group[i]).at[:, EXPERT_LANE0:EXPERT_LANE0 + N_EXPERTS].set(w_expert_router[i])
    wr_hi, wr_lo = _split2(w_router)
    b_router = jnp.zeros((1, ROUTER_LANES), F32)
    b_router = b_router.at[0, :N_GROUPS].set(b_group[i]).at[0, EXPERT_LANE0:EXPERT_LANE0 + N_EXPERTS].set(b_expert_router[i])
    return dict(
        norm_mix=vec(norm_mix), w_in=w_in[i].astype(BF16), mu=vec(mu), w0=vec(w0), a0=vec(a0),
        w_kv_t=w_in[i][:, w_in.shape[2] - 2 * ((w_in.shape[2] - mu.shape[1]) // 3):].T.astype(BF16),
        w2p=lora(w2[i], 0), a2p=lora(a2[i], DECAY_LORA), g2p=lora(g2[i], DECAY_LORA + AAA_LORA),
        k_k=vec(k_k), k_a=vec(k_a), r_k=vec(r_k), ln_x_w=vec(ln_x_w), ln_x_b=vec(ln_x_b),
        bd=jnp.asarray(head[:, None] == head[None, :], BF16),
        attn_gain=vec(attn_gain), w_out=w_out[i].astype(BF16), norm_ffn=vec(norm_ffn),
        wr_hi=wr_hi, wr_lo=wr_lo, b_router=b_router,
        w_gate=w_gate[i].astype(BF16), w_up=w_up[i].astype(BF16), w_down=w_down[i].astype(BF16),
        norm_ple=vec(norm_ple), w_ple=w_ple[i].astype(BF16), w_ple_gate=w_ple_gate[i].astype(BF16),
        norm_final=norm_final.reshape(1, -1),
    )


def _tile(n, want):
    t = min(n, want)
    assert n % t == 0, (n, t)
    return t


def _ffn_and_tail(x, y, g, bonus, attn, p, prm, rw):
    n = x.shape[0]
    x1, hn, comb, sel = _post(x, y, g, bonus, attn, prm, rw, _tile(n, 512))
    moe = _moe(hn, comb, sel, prm["w_gate"], prm["w_up"], prm["w_down"], _tile(n, 1024))
    return _tail(x1, moe, p, prm, _tile(n, 512))


def kernel(x_prompt, x_sample, state_wkv, state_shift, cache_k_win, cache_v_win, p_prompt, p_sample, norm_mix, w_in, mu, w0, w2, a0, a2, g2, k_k, k_a, r_k, ln_x_w, ln_x_b, attn_gain, w_out, norm_ffn, w_group, b_group, w_expert_router, b_expert_router, w_gate, w_up, w_down, norm_ple, w_ple, w_ple_gate, norm_final):
    depth = norm_mix.shape[0]
    assert depth == 1, "single-layer step"
    bp, s_len, d = x_prompt.shape
    bs, t_len, _ = x_sample.shape
    rw = w0.shape[1]
    n_rw_heads = rw // HEAD_DIM
    rw_cols = mu.shape[1]
    at_width = (w_in.shape[2] - rw_cols) // 3
    n_at_heads = at_width // HEAD_DIM
    win = cache_k_win.shape[2]
    assert s_len % (max(DILATIONS) * ATT_BLOCK) == 0 and s_len >= win and t_len == 8 and bs % 8 == 0
    assert all(wd // dl == ATT_BLOCK for wd, dl in zip(WINDOWS, DILATIONS)) and win >= max(WINDOWS)

    prm = _layer_params(0, norm_mix, w_in, mu, w0, w2, a0, a2, g2, k_k, k_a, r_k, ln_x_w, ln_x_b, attn_gain, w_out,
                        norm_ffn, w_group, b_group, w_expert_router, b_expert_router, w_gate, w_up, w_down,
                        norm_ple, w_ple, w_ple_gate, norm_final)

    n_p = bp * s_len
    xp = x_prompt.reshape(n_p, d)
    pr, q, k, v = _in_proj(xp, prm["norm_mix"], prm["w_in"], rw_cols, at_width, _tile(n_p, 512), True)
    tm = _tile(s_len, 512)
    r_, lw, km, vv, kk, kka, g, bonus = _rwkv_prep(pr, pr, prm, rw, tm, 1, s_len // tm)
    ct = _tile(s_len, 512)
    y, wkv_p = _wkv_prompt(bp, n_rw_heads, (r_, lw, km, vv, kk, kka), ct)
    attn = _attn_prompt(q, k, v, bp, n_at_heads)
    y_prompt = _ffn_and_tail(xp, y, g, bonus, attn, p_prompt[0].reshape(n_p, -1), prm, rw).reshape(bp, s_len, d)
    shift_p = pr.reshape(bp, s_len, rw_cols)[:, -1]
    keep = min(win, s_len)
    kv_t = _win_proj(x_prompt, prm["norm_mix"], prm["w_kv_t"], keep, _tile(keep, 512))
    window = lambda z: jnp.transpose(z.reshape(bp, n_at_heads, HEAD_DIM, keep), (0, 3, 1, 2))
    kwin_p, vwin_p = window(kv_t[:, :at_width]), window(kv_t[:, at_width:])

    n_s = bs * t_len
    xs = jnp.swapaxes(x_sample, 0, 1).reshape(n_s, d)
    ps = jnp.swapaxes(p_sample[0], 0, 1).reshape(n_s, -1)
    pr_s, q_s, k_s, v_s = _in_proj(xs, prm["norm_mix"], prm["w_in"], rw_cols, at_width, _tile(n_s, 256), False)
    r_, lw, km, vv, kk, kka, g, bonus = _rwkv_prep(pr_s, state_shift[0], prm, rw, n_s, bs, 1)
    y, wkv_s = _wkv_sample(t_len, bs, n_rw_heads, (r_, lw, km, vv, kk, kka), state_wkv[0], 8)
    to_t = lambda c: jnp.transpose(c[0], (0, 2, 3, 1)).reshape(bs, at_width, win)
    from_t = lambda c: jnp.transpose(c.reshape(bs, n_at_heads, HEAD_DIM, win), (0, 3, 1, 2))[None]
    at_s, kwin_s, vwin_s = _attn_sample(q_s, k_s, v_s, to_t(cache_k_win), to_t(cache_v_win), t_len, bs, n_at_heads)
    y_s = _ffn_and_tail(xs, y, g, bonus, at_s, ps, prm, rw)
    y_sample = jnp.swapaxes(y_s.reshape(t_len, bs, d), 0, 1)
    shift_s = pr_s[(t_len - 1) * bs:]

    return (y_prompt, y_sample, wkv_p[None], shift_p[None], kwin_p[None], vwin_p[None],
            wkv_s[None], shift_s[None], from_t(kwin_s), from_t(vwin_s))
```

```python
import functools

import numpy as np
import jax
import jax.numpy as jnp
from jax import lax
from jax.experimental import pallas as pl
from jax.experimental.pallas import tpu as pltpu

F32 = jnp.float32
BF16 = jnp.bfloat16

LANES = 128
HEAD_DIM = 64
PAIR = 2 * HEAD_DIM
DECAY_LORA = 32
AAA_LORA = 32
GATE_LORA = 64
GN_EPS = 64e-5
RMS_EPS = 1e-6
WINDOWS = (128, 512, 2048)
DILATIONS = (1, 4, 16)
ATT_BLOCK = 128
N_GROUPS = 4
EXPERTS_PER_GROUP = 8
N_EXPERTS = N_GROUPS * EXPERTS_PER_GROUP
ROUTER_LANES = 128
EXPERT_LANE0 = N_GROUPS
WKV_CHUNK = 64
VMEM_LIMIT = 56 * 1024 * 1024


def _cparams(n_axes, vmem_limit=VMEM_LIMIT):
    return pltpu.CompilerParams(dimension_semantics=("arbitrary",) * n_axes, vmem_limit_bytes=vmem_limit)


def _full(shape):
    nd = len(shape)
    return pl.BlockSpec(shape, lambda *_: (0,) * nd)


def _bdot(a, b):
    return jnp.dot(a.astype(BF16), b.astype(BF16), preferred_element_type=F32)


def _split2(a):
    hi = a.astype(BF16)
    lo = (a - hi.astype(F32)).astype(BF16)
    return hi, lo


def _hilo_dot(a, b_bf16):
    hi, lo = _split2(a)
    return (jnp.dot(hi, b_bf16, preferred_element_type=F32)
            + jnp.dot(lo, b_bf16, preferred_element_type=F32))


def _rms(x, g):
    return x * lax.rsqrt(jnp.mean(x * x, axis=-1, keepdims=True) + RMS_EPS) * g


def _in_proj_kernel(x_ref, g_ref, w_ref, pr_ref, q_ref, k_ref, v_ref, *, rw_cols, at_width, slabs):
    h = _rms(x_ref[...], g_ref[...]).astype(BF16)
    pr_ref[...] = jnp.dot(h, w_ref[:, :rw_cols], preferred_element_type=F32)
    for i, o_ref in enumerate((q_ref, k_ref, v_ref)):
        lo = rw_cols + i * at_width
        z = jnp.dot(h, w_ref[:, lo:lo + at_width], preferred_element_type=F32)
        if slabs:
            for s in range(at_width // PAIR):
                o_ref[s] = z[:, s * PAIR:(s + 1) * PAIR]
        else:
            o_ref[...] = z


def _in_proj(x, g, w_bf16, rw_cols, at_width, tm, slabs):
    n, d = x.shape
    row = lambda w: pl.BlockSpec((tm, w), lambda i: (i, 0))
    if slabs:
        qkv_spec = pl.BlockSpec((at_width // PAIR, tm, PAIR), lambda i: (0, i, 0))
        qkv_shape = jax.ShapeDtypeStruct((at_width // PAIR, n, PAIR), F32)
    else:
        qkv_spec, qkv_shape = row(at_width), jax.ShapeDtypeStruct((n, at_width), F32)
    return pl.pallas_call(
        functools.partial(_in_proj_kernel, rw_cols=rw_cols, at_width=at_width, slabs=slabs),
        grid=(n // tm,),
        in_specs=[row(d), _full(g.shape), _full(w_bf16.shape)],
        out_specs=[row(rw_cols)] + [qkv_spec] * 3,
        out_shape=[jax.ShapeDtypeStruct((n, rw_cols), F32)] + [qkv_shape] * 3,
        compiler_params=_cparams(1),
        name="in_proj",
    )(x, g, w_bf16)


def _win_proj_kernel(x_ref, g_ref, wt_ref, o_ref):
    h = _rms(x_ref[0], g_ref[...]).astype(BF16)
    o_ref[0] = lax.dot_general(wt_ref[...], h, (((1,), (1,)), ((), ())), preferred_element_type=F32)


def _win_proj(x, g, wt_bf16, keep, tm):
    seqs, s_len, d = x.shape
    cols = wt_bf16.shape[0]
    first = (s_len - keep) // tm
    return pl.pallas_call(
        _win_proj_kernel,
        grid=(seqs, keep // tm),
        in_specs=[pl.BlockSpec((1, tm, d), lambda b, j: (b, first + j, 0)), _full(g.shape), _full(wt_bf16.shape)],
        out_specs=pl.BlockSpec((1, cols, tm), lambda b, j: (b, 0, j)),
        out_shape=jax.ShapeDtypeStruct((seqs, cols, keep), F32),
        compiler_params=_cparams(2),
        name="win_proj",
    )(x, g, wt_bf16)


def _rwkv_prep_kernel(pr_ref, bnd_ref, mu_ref, w0_ref, w2_ref, a0_ref, a2_ref, g2_ref, kk_ref, ka_ref, rk_ref,
                      bd_ref, r_o, lw_o, k_o, v_o, kk_o, kka_o, g_o, bonus_o, *, rw, shift_rows, tiles_per_seq):
    pr = pr_ref[...]
    tm = pr.shape[0]
    if shift_rows == 1:
        first = pl.program_id(0) % tiles_per_seq == 0
        prev_row = jnp.where(first, 0.0, bnd_ref[7:8, :])
        row = lax.broadcasted_iota(jnp.int32, pr.shape, 0)
        prev = jnp.where(row == 0, prev_row, pltpu.roll(pr, 1, 0))
    else:
        prev = jnp.concatenate([bnd_ref[...], pr[:tm - shift_rows]], axis=0)
    xm = pr + (prev - pr) * mu_ref[...]
    xr, xk, xv, xl = xm[:, :rw], xm[:, rw:2 * rw], xm[:, 2 * rw:3 * rw], xm[:, 3 * rw:]
    bd = bd_ref[...]
    z = -(w0_ref[...] + _bdot(jnp.tanh(xl), w2_ref[...]))
    softplus = jnp.maximum(z, 0.0) + jnp.log(1.0 + jnp.exp(-jnp.abs(z)))
    lw = -jnp.exp(-softplus - 0.5)
    a = jax.nn.sigmoid(a0_ref[...] + _bdot(xl, a2_ref[...]))
    g = _bdot(jax.nn.sigmoid(xl), g2_ref[...])
    kk = xk * kk_ref[...]
    kk = kk / jnp.maximum(jnp.sqrt(_hilo_dot(kk * kk, bd)), 1e-12)
    kmod = xk * (1.0 + (a - 1.0) * ka_ref[...])
    bonus = _hilo_dot(xr * kmod * rk_ref[...], bd) * xv
    r_o[...] = xr
    lw_o[...] = lw
    k_o[...] = kmod
    v_o[...] = xv
    kk_o[...] = kk
    kka_o[...] = kk * a
    g_o[...] = g
    bonus_o[...] = bonus


def _rwkv_prep(pr, bnd, prm, rw, tm, shift_rows, tiles_per_seq):
    n, c = pr.shape
    row = lambda w: pl.BlockSpec((tm, w), lambda i: (i, 0))
    if shift_rows == 1:
        bnd_spec = pl.BlockSpec((8, c), lambda i: (jnp.maximum(i * (tm // 8) - 1, 0), 0))
    else:
        bnd_spec = _full(bnd.shape)
    params = [prm[k] for k in ("mu", "w0", "w2p", "a0", "a2p", "g2p", "k_k", "k_a", "r_k", "bd")]
    return pl.pallas_call(
        functools.partial(_rwkv_prep_kernel, rw=rw, shift_rows=shift_rows, tiles_per_seq=tiles_per_seq),
        grid=(n // tm,),
        in_specs=[row(c), bnd_spec] + [_full(p.shape) for p in params],
        out_specs=[row(rw)] * 8,
        out_shape=[jax.ShapeDtypeStruct((n, rw), F32)] * 8,
        compiler_params=_cparams(1),
        name="rwkv_prep",
    )(pr, bnd, *params)


def _cumsum_rows(x):
    n = x.shape[0]
    row = lax.broadcasted_iota(jnp.int32, x.shape, 0)
    s = 1
    while s < n:
        x = x + jnp.where(row >= s, pltpu.roll(x, s, 0), 0.0)
        s *= 2
    return x


def _lockstep(gens):
    results = [None] * len(gens)
    live = list(range(len(gens)))
    while live:
        still = []
        for i in live:
            try:
                next(gens[i])
                still.append(i)
            except StopIteration as stop:
                results[i] = stop.value
        live = still
    return results


def _wkv_head_chunk(pt, rt, kt, qt, kt_end, qt_end, vh, p_end, s_h, masks):
    strict, incl, eye = masks
    c = pt.shape[0]
    nt = lambda a_, b_: lax.dot_general(a_, b_, (((1,), (1,)), ((), ())), preferred_element_type=F32)
    nn = lambda a_, b_: jnp.dot(a_, b_, preferred_element_type=F32)
    tn = lambda a_, b_: lax.dot_general(a_, b_, (((0,), (0,)), ((), ())), preferred_element_type=F32)
    lhs = jnp.concatenate([pt, rt], axis=0)
    gq = nt(lhs, qt)
    gk = nt(lhs, kt)
    yield
    l_qp = jnp.where(strict, gq[:c], 0.0)
    l_kp = jnp.where(strict, gk[:c], 0.0)
    a_rq = jnp.where(incl, gq[c:], 0.0)
    a_rk = jnp.where(incl, gk[c:], 0.0)
    t_inv = eye - l_qp
    pw = -l_qp
    lv_av = nn(jnp.concatenate([l_kp, a_rk], axis=0), vh)
    for _ in range(int(np.log2(c)) - 1):
        pw = nn(pw, pw)
        yield
        t_new = nn(pw, t_inv)
        yield
        t_inv = t_inv + t_new
    w_mat = nn(t_inv, pt)
    u_v = nn(t_inv, lv_av[:c])
    yield
    ws = nt(jnp.concatenate([w_mat, rt], axis=0), s_h)
    yield
    u = u_v + ws[:c]
    au = nn(a_rq, u)
    upd = tn(jnp.concatenate([vh, -u], axis=0), jnp.concatenate([kt_end, qt_end], axis=0))
    yield
    return ws[c:] + lv_av[c:] - au, s_h * p_end + upd


def _pair_stack(z, first):
    return jnp.concatenate([jnp.where(first, z, 0.0), jnp.where(first, 0.0, z)], axis=0).astype(BF16)


def _wkv_pair_static(slabs, masks):
    r, lw, k, v, kk, kka = slabs
    first, strict, incl, eye = masks
    c = r.shape[0]
    n2 = 2 * c
    nt = lambda a_, b_: lax.dot_general(a_, b_, (((1,), (1,)), ((), ())), preferred_element_type=F32)
    nn = lambda a_, b_: jnp.dot(a_, b_, preferred_element_type=F32)
    cum = _cumsum_rows(lw)
    cum_last = cum[c - 1:c, :]
    d_end = jnp.exp(cum_last - cum)
    p_inv = jnp.exp(-cum)
    ps = _pair_stack(kk * jnp.exp(cum - lw), first)
    rs = _pair_stack(r * jnp.exp(cum), first)
    qs = _pair_stack(kka * p_inv, first)
    ks = _pair_stack(k * p_inv, first)
    vs = _pair_stack(v, first)
    ends = jnp.concatenate([_pair_stack(k * d_end, first), _pair_stack(kka * d_end, first)], axis=0)
    g = nt(jnp.concatenate([ps, rs], axis=0), jnp.concatenate([qs, ks], axis=0))
    yield
    l_qp = jnp.where(strict, g[:n2, :n2], 0.0)
    l_kp = jnp.where(strict, g[:n2, n2:], 0.0).astype(BF16)
    a_rq = jnp.where(incl, g[n2:, :n2], 0.0).astype(BF16)
    a_rk = jnp.where(incl, g[n2:, n2:], 0.0).astype(BF16)
    t_inv = eye - l_qp
    pw = (-l_qp).astype(BF16)
    lv_av = nn(jnp.concatenate([l_kp, a_rk], axis=0), vs)
    pw2 = nn(pw, pw)
    yield
    lv, av = lv_av[:n2].astype(BF16), lv_av[n2:]
    pw = pw2.astype(BF16)
    n_it = int(np.log2(c)) - 1
    for i in range(n_it):
        if i < n_it - 1:
            out = nn(pw, jnp.concatenate([pw, t_inv.astype(BF16)], axis=1))
            yield
            pw = out[:, :n2].astype(BF16)
            t_inv = t_inv + out[:, n2:]
        else:
            out = nn(pw, t_inv.astype(BF16))
            yield
            t_inv = t_inv + out
    tp = nn(t_inv.astype(BF16), jnp.concatenate([ps, lv], axis=1))
    yield
    return dict(w=tp[:, :PAIR].astype(BF16), u_v=tp[:, PAIR:], rs=rs, vs=vs, a_rq=a_rq, av=av, ends=ends,
                p_end=jnp.exp(cum_last))


def _wkv_pair_step(st, s_bd):
    nt = lambda a_, b_: lax.dot_general(a_, b_, (((1,), (1,)), ((), ())), preferred_element_type=F32)
    tn = lambda a_, b_: lax.dot_general(a_, b_, (((0,), (0,)), ((), ())), preferred_element_type=F32)
    n2 = st["w"].shape[0]
    ws = nt(jnp.concatenate([st["w"], st["rs"]], axis=0), s_bd.astype(BF16))
    yield
    u = st["u_v"] + ws[:n2]
    upd = tn(jnp.concatenate([st["vs"], (-u).astype(BF16)], axis=0), st["ends"])
    au = jnp.dot(st["a_rq"], u.astype(BF16), preferred_element_type=F32)
    yield
    y_bd = ws[n2:] + st["av"] - au
    return y_bd[:n2 // 2] + y_bd[n2 // 2:], s_bd * st["p_end"] + upd


def _wkv_prompt_kernel(r_ref, lw_ref, k_ref, v_ref, kk_ref, kka_ref, y_ref, sT_ref, s_scr, *, chunks, tiles_per_seq):
    j = pl.program_id(0) % tiles_per_seq
    n_pairs = s_scr.shape[0]
    c = WKV_CHUNK

    @pl.when(j == 0)
    def _():
        s_scr[...] = jnp.zeros_like(s_scr)

    lane = lax.broadcasted_iota(jnp.int32, (c, PAIR), 1)
    ri = lax.broadcasted_iota(jnp.int32, (2 * c, 2 * c), 0)
    ci = lax.broadcasted_iota(jnp.int32, (2 * c, 2 * c), 1)
    rt_, ct_ = ri % c, ci % c
    masks = (lane < HEAD_DIM, rt_ > ct_, rt_ >= ct_, (ri == ci).astype(F32))

    gens = []
    for ch in range(chunks):
        rows = slice(ch * c, (ch + 1) * c)
        for p in range(n_pairs):
            lanes = slice(p * PAIR, (p + 1) * PAIR)
            slabs = [ref[rows, lanes] for ref in (r_ref, lw_ref, k_ref, v_ref, kk_ref, kka_ref)]
            gens.append(_wkv_pair_static(slabs, masks))
    static = _lockstep(gens)
    states = [s_scr[p] for p in range(n_pairs)]
    for ch in range(chunks):
        outs = _lockstep([_wkv_pair_step(static[ch * n_pairs + p], states[p]) for p in range(n_pairs)])
        for p, (y, s_new) in enumerate(outs):
            states[p] = s_new
            y_ref[ch * c:(ch + 1) * c, p * PAIR:(p + 1) * PAIR] = y
    for p in range(n_pairs):
        s_scr[p] = states[p]

    @pl.when(j == tiles_per_seq - 1)
    def _():
        for p in range(n_pairs):
            sT_ref[0, 2 * p] = s_scr[p, :HEAD_DIM, :HEAD_DIM]
            sT_ref[0, 2 * p + 1] = s_scr[p, HEAD_DIM:, HEAD_DIM:]


def _wkv_prompt(seqs, n_heads, arrays, ct):
    n, w = arrays[0].shape
    tiles_per_seq = n // seqs // ct
    row = pl.BlockSpec((ct, w), lambda i: (i, 0))
    st_shape = (seqs, n_heads, HEAD_DIM, HEAD_DIM)
    return pl.pallas_call(
        functools.partial(_wkv_prompt_kernel, chunks=ct // WKV_CHUNK, tiles_per_seq=tiles_per_seq),
        grid=(n // ct,),
        in_specs=[row] * 6,
        out_specs=[row, pl.BlockSpec((1,) + st_shape[1:], lambda i: (i // tiles_per_seq, 0, 0, 0))],
        out_shape=[jax.ShapeDtypeStruct((n, w), F32), jax.ShapeDtypeStruct(st_shape, F32)],
        scratch_shapes=[pltpu.VMEM((n_heads // 2, PAIR, PAIR), F32)],
        compiler_params=_cparams(1),
        name="wkv_prompt",
    )(*arrays)


def _wkv_sample_kernel(r_ref, lw_ref, k_ref, v_ref, kk_ref, kka_ref, s0_ref, y_ref, sT_ref, *, seqs_per_step):
    n_heads = s0_ref.shape[1]
    c = r_ref.shape[0]
    ri = lax.broadcasted_iota(jnp.int32, (c, c), 0)
    ci = lax.broadcasted_iota(jnp.int32, (c, c), 1)
    masks = (ri > ci, ri >= ci, (ri == ci).astype(F32))
    gens = []
    for s in range(seqs_per_step):
        r, lw, k, v, kk, kka = [ref[:, s, 0, :] for ref in (r_ref, lw_ref, k_ref, v_ref, kk_ref, kka_ref)]
        cum = _cumsum_rows(lw)
        cum_last = cum[c - 1:c, :]
        p_inv = jnp.exp(-cum)
        d_end = jnp.exp(cum_last - cum)
        ops = (kk * jnp.exp(cum - lw), r * jnp.exp(cum), k * p_inv, kka * p_inv, k * d_end, kka * d_end, v,
               jnp.exp(cum_last))
        for h in range(n_heads):
            hs = slice(h * HEAD_DIM, (h + 1) * HEAD_DIM)
            gens.append(_wkv_head_chunk(*[z[:, hs] for z in ops], s0_ref[s, h], masks))
    outs = _lockstep(gens)
    for s in range(seqs_per_step):
        heads = outs[s * n_heads:(s + 1) * n_heads]
        y_ref[:, s, 0, :] = jnp.concatenate([y for y, _ in heads], axis=1)
        for h, (_, s_new) in enumerate(heads):
            sT_ref[s, h] = s_new


def _wkv_sample(t_len, seqs, n_heads, arrays, s0, seqs_per_step):
    w = arrays[0].shape[-1]
    arrays = [a.reshape(t_len, seqs, 1, w) for a in arrays]
    tok = pl.BlockSpec((t_len, seqs_per_step, 1, w), lambda i: (0, i, 0, 0))
    st = pl.BlockSpec((seqs_per_step, n_heads, HEAD_DIM, HEAD_DIM), lambda i: (i, 0, 0, 0))
    y, s_t = pl.pallas_call(
        functools.partial(_wkv_sample_kernel, seqs_per_step=seqs_per_step),
        grid=(seqs // seqs_per_step,),
        in_specs=[tok] * 6 + [st],
        out_specs=[tok, st],
        out_shape=[jax.ShapeDtypeStruct((t_len, seqs, 1, w), F32), jax.ShapeDtypeStruct(s0.shape, F32)],
        compiler_params=_cparams(1),
        name="wkv_sample",
    )(*arrays, s0)
    return y.reshape(t_len * seqs, w), s_t


ATT_SPAN = max(DILATIONS) * ATT_BLOCK


def _attn_prompt_kernel(q_ref, kp_ref, kc_ref, vp_ref, vc_ref, o_ref, kx, vx, ob, lb, bias_scr, *, n_heads,
                        tiles_per_iter):
    slab, blk = pl.program_id(1), pl.program_id(2)
    span = ATT_SPAN
    kx[:span] = kp_ref[0]
    kx[span:] = kc_ref[0]
    vx[:span] = vp_ref[0]
    vx[span:] = vc_ref[0]
    qi = lax.broadcasted_iota(jnp.int32, (ATT_BLOCK, 2 * ATT_BLOCK), 0)
    ki = lax.broadcasted_iota(jnp.int32, (ATT_BLOCK, 2 * ATT_BLOCK), 1)
    steps = ATT_BLOCK + qi - ki
    band = (steps >= 0) & (steps <= ATT_BLOCK)
    stepsf = steps.astype(F32)
    first = lax.broadcasted_iota(jnp.int32, (ATT_BLOCK, PAIR), 1) < HEAD_DIM
    log2e = float(np.log2(np.e))
    scale = HEAD_DIM ** -0.5 * log2e
    for h2 in range(2):
        alibi = jnp.exp2(jnp.zeros_like(stepsf) - (8.0 / n_heads) * (2 * slab + 1 + h2).astype(F32)) * (stepsf * log2e)
        for br, dil in enumerate(DILATIONS):
            bias_scr[0, br, h2] = jnp.where(band, alibi * float(dil), jnp.inf)
            bias_scr[1, br, h2] = jnp.where(band & (ki >= ATT_BLOCK), alibi * float(dil), jnp.inf)

    def head(q_h, k_t, v_t, bias):
        s = lax.dot_general(q_h, k_t, (((1,), (1,)), ((), ())), preferred_element_type=F32)
        yield
        s = s - bias
        m = jnp.max(s, axis=-1, keepdims=True)
        e = jnp.exp2(s - m)
        den = jnp.sum(e, axis=-1, keepdims=True)
        o = jnp.dot(e.astype(BF16), v_t, preferred_element_type=F32)
        yield
        return o / den, m + jnp.log2(den)

    def body(it, carry):
        gens, where_to = [], []
        for u in range(tiles_per_iter):
            t = it * tiles_per_iter + u
            for br, dil in enumerate(DILATIONS):
                sub = t // dil
                base = t % dil + dil * ATT_BLOCK * sub
                rows_q = pl.ds(base, ATT_BLOCK, stride=dil)
                rows_kv = pl.ds(span + base - dil * ATT_BLOCK, 2 * ATT_BLOCK, stride=dil)
                q_t = q_ref[0, rows_q, :] * scale
                k_t = kx[rows_kv, :].astype(BF16)
                v_t = vx[rows_kv, :].astype(BF16)
                at_start = jnp.where((blk > 0) | (sub > 0), 0, 1)
                for h2 in range(2):
                    q_h = jnp.where(first if h2 == 0 else ~first, q_t, 0.0).astype(BF16)
                    gens.append(head(q_h, k_t, v_t, bias_scr[at_start, br, h2]))
                where_to.append((br, rows_q))
        res = _lockstep(gens)
        for i, (br, rows_q) in enumerate(where_to):
            (o0, l0), (o1, l1) = res[2 * i], res[2 * i + 1]
            ob[br, rows_q, :] = jnp.where(first, o0, o1)
            lb[br, rows_q, :] = jnp.where(first, l0, l1)
        return carry

    lax.fori_loop(0, span // ATT_BLOCK // tiles_per_iter, body, 0)
    ls = [lb[br] for br in range(len(DILATIONS))]
    m = functools.reduce(jnp.maximum, ls)
    es = [jnp.exp2(l - m) for l in ls]
    o_ref[0] = sum(e * ob[br] for br, e in enumerate(es)) / sum(es)


def _attn_prompt(q, k, v, seqs, n_heads):
    n_slabs, n, _ = q.shape
    n_blk = n // seqs // ATT_SPAN
    cur = pl.BlockSpec((1, ATT_SPAN, PAIR), lambda b, s, i: (s, b * n_blk + i, 0))
    prev = pl.BlockSpec((1, ATT_SPAN, PAIR), lambda b, s, i: (s, b * n_blk + jnp.maximum(i - 1, 0), 0))
    return pl.pallas_call(
        functools.partial(_attn_prompt_kernel, n_heads=n_heads, tiles_per_iter=2),
        grid=(seqs, n_slabs, n_blk),
        in_specs=[cur, prev, cur, prev, cur],
        out_specs=cur,
        out_shape=jax.ShapeDtypeStruct(q.shape, F32),
        scratch_shapes=[pltpu.VMEM((2 * ATT_SPAN, PAIR), F32)] * 2
                       + [pltpu.VMEM((len(DILATIONS), ATT_SPAN, PAIR), F32)] * 2
                       + [pltpu.VMEM((2, len(DILATIONS), 2, ATT_BLOCK, 2 * ATT_BLOCK), F32)],
        compiler_params=_cparams(3),
        name="attn_prompt",
    )(q, k, k, v, v)


def _attn_sample_kernel(q_ref, kn_ref, vn_ref, kc_ref, vc_ref, o_ref, ko_ref, vo_ref, *, n_heads):
    t_len = q_ref.shape[0]
    w, win = kc_ref.shape[1], kc_ref.shape[2]
    rows = n_heads * t_len
    q = q_ref[:, 0, 0, :]
    kn = kn_ref[:, 0, 0, :]
    vn = vn_ref[:, 0, 0, :]
    nt = lambda a_, b_: lax.dot_general(a_, b_, (((1,), (1,)), ((), ())), preferred_element_type=F32)
    tn = lambda a_, b_: lax.dot_general(a_, b_, (((0,), (0,)), ((), ())), preferred_element_type=F32)

    pad_rows = 16
    place = (lax.broadcasted_iota(jnp.int32, (pad_rows, LANES), 1)
             == lax.broadcasted_iota(jnp.int32, (pad_rows, LANES), 0) + (LANES - t_len)).astype(BF16)
    lane = lax.broadcasted_iota(jnp.int32, (LANES, LANES), 1)

    def transposed_tail(z):
        zp = jnp.concatenate([z, jnp.zeros((pad_rows - t_len, w), F32)], axis=0)
        hi = zp.astype(BF16)
        r1 = zp - hi.astype(F32)
        mid = r1.astype(BF16)
        lo = (r1 - mid.astype(F32)).astype(BF16)
        return tn(hi, place) + tn(mid, place) + tn(lo, place)

    for src, new, dst in ((kc_ref, kn, ko_ref), (vc_ref, vn, vo_ref)):
        tail = transposed_tail(new)
        for r0 in range(0, w, LANES):
            rolled = pltpu.roll(src[0, r0:r0 + LANES, :], win - t_len, 1)
            dst[0, r0:r0 + LANES, :win - LANES] = rolled[:, :win - LANES]
            dst[0, r0:r0 + LANES, win - LANES:] = jnp.where(lane < LANES - t_len, rolled[:, win - LANES:],
                                                             tail[r0:r0 + LANES])

    qrep = jnp.concatenate([q] * n_heads, axis=0)
    rh = lax.broadcasted_iota(jnp.int32, (rows, w), 0) // t_len
    ch = lax.broadcasted_iota(jnp.int32, (rows, w), 1) // HEAD_DIM
    qexp = jnp.where(rh == ch, qrep, 0.0).astype(BF16)
    scale = HEAD_DIM ** -0.5

    def weights(n_keys, key_pos0):
        ri = lax.broadcasted_iota(jnp.int32, (rows, n_keys), 0)
        ki = lax.broadcasted_iota(jnp.int32, (rows, n_keys), 1)
        dist = win + ri % t_len - (key_pos0 + ki)
        cnt = jnp.zeros((rows, n_keys), F32)
        for wdw, dil in zip(WINDOWS, DILATIONS):
            ok = (dist >= 0) & (dist % dil == 0) & (dist <= wdw)
            cnt = cnt + jnp.where(ok, 1.0, 0.0)
        slope = jnp.exp2(-8.0 * (ri // t_len + 1).astype(F32) / n_heads)
        return cnt, slope * dist.astype(F32)

    cnt_c, bias_c = weights(win, 0)
    cnt_n, bias_n = weights(t_len, win)
    s_c = jnp.where(cnt_c > 0, jnp.dot(qexp, kc_ref[0].astype(BF16), preferred_element_type=F32) * scale - bias_c,
                    -jnp.inf)
    s_n = jnp.where(cnt_n > 0, nt(qexp, kn.astype(BF16)) * scale - bias_n, -jnp.inf)
    m = jnp.maximum(jnp.max(s_c, axis=-1, keepdims=True), jnp.max(s_n, axis=-1, keepdims=True))
    e_c = cnt_c * jnp.exp(s_c - m)
    e_n = cnt_n * jnp.exp(s_n - m)
    den = jnp.sum(e_c, axis=-1, keepdims=True) + jnp.sum(e_n, axis=-1, keepdims=True)
    acc = (nt(e_c.astype(BF16), vc_ref[0].astype(BF16)) + jnp.dot(e_n, vn, preferred_element_type=F32)) / den
    acc = jnp.where(rh == ch, acc, 0.0)
    out = acc[:t_len]
    for h in range(1, n_heads):
        out = out + acc[h * t_len:(h + 1) * t_len]
    o_ref[:, 0, 0, :] = out


def _attn_sample(q, k, v, kc, vc, t_len, seqs, n_heads):
    _, w, win = kc.shape
    tok = pl.BlockSpec((t_len, 1, 1, w), lambda b: (0, b, 0, 0))
    cache = pl.BlockSpec((1, w, win), lambda b: (b, 0, 0))
    v4 = lambda a: a.reshape(t_len, seqs, 1, w)
    o, ko, vo = pl.pallas_call(
        functools.partial(_attn_sample_kernel, n_heads=n_heads),
        grid=(seqs,),
        in_specs=[tok, tok, tok, cache, cache],
        out_specs=[tok, cache, cache],
        out_shape=[jax.ShapeDtypeStruct((t_len, seqs, 1, w), F32)] + [jax.ShapeDtypeStruct(kc.shape, F32)] * 2,
        compiler_params=_cparams(1),
        name="attn_sample",
    )(v4(q), v4(k), v4(v), kc, vc)
    return o.reshape(t_len * seqs, w), ko, vo


def _post_kernel(x_ref, y_ref, g_ref, bonus_ref, at_ref, lnw_ref, lnb_ref, gain_ref, wout_ref, nffn_ref, wr_hi_ref,
                 wr_lo_ref, br_ref, bd_ref, x1_ref, hn_ref, comb_ref, sel_ref, *, attn_slabs, rw):

    bd = bd_ref[...]
    y = y_ref[...]
    mean = _hilo_dot(y, bd) * (1.0 / HEAD_DIM)
    d = y - mean
    var = _hilo_dot(d * d, bd) * (1.0 / HEAD_DIM)
    yn = d * lax.rsqrt(var + GN_EPS) * lnw_ref[...] + lnb_ref[...]
    rw_out = (yn + bonus_ref[...]) * g_ref[...]

    if attn_slabs:
        at = jnp.concatenate([at_ref[s] for s in range(at_ref.shape[0])], axis=1)
    else:
        at = at_ref[...]
    at_out = _rms(at, gain_ref[...])

    x1 = (x_ref[...] + jnp.dot(rw_out.astype(BF16), wout_ref[:rw, :], preferred_element_type=F32)
          + jnp.dot(at_out.astype(BF16), wout_ref[rw:, :], preferred_element_type=F32))
    x1_ref[...] = x1
    hn = _rms(x1, nffn_ref[...])
    hn_ref[...] = hn.astype(BF16)

    hi, lo = _split2(hn)
    logits = (jnp.dot(hi, wr_hi_ref[...], preferred_element_type=F32)
              + jnp.dot(lo, wr_hi_ref[...], preferred_element_type=F32)
              + jnp.dot(hi, wr_lo_ref[...], preferred_element_type=F32)) + br_ref[...]
    lane = lax.broadcasted_iota(jnp.int32, logits.shape, 1)
    big = jnp.int32(ROUTER_LANES)
    first_max = lambda z: jnp.min(jnp.where(z == jnp.max(z, axis=-1, keepdims=True), lane, big), axis=-1, keepdims=True)
    gl = jnp.where(lane < N_GROUPS, logits, -jnp.inf)
    gsel = first_max(gl)
    gw = 1.0 / jnp.sum(jnp.exp(gl - jnp.max(gl, axis=-1, keepdims=True)), axis=-1, keepdims=True)
    e_lane = lane - EXPERT_LANE0
    in_group = (e_lane >= 0) & (e_lane < N_EXPERTS) & (e_lane // EXPERTS_PER_GROUP == gsel)
    el = jnp.where(in_group, logits, -jnp.inf)
    m1 = jnp.max(el, axis=-1, keepdims=True)
    i1 = first_max(el)
    el2 = jnp.where(lane == i1, -jnp.inf, el)
    m2 = jnp.max(el2, axis=-1, keepdims=True)
    i2 = first_max(el2)
    e2 = jnp.exp(m2 - m1)
    w1 = gw / (1.0 + e2)
    comb_ref[...] = jnp.where(lane == i1, w1, 0.0) + jnp.where(lane == i2, w1 * e2, 0.0)
    sel_ref[...] = jnp.where((lane == i1) | (lane == i2), 1.0, 0.0)


def _post(x, y, g, bonus, attn, prm, rw, tm):
    n, d = x.shape
    row = lambda w: pl.BlockSpec((tm, w), lambda i: (i, 0))
    attn_slabs = attn.ndim == 3
    at_spec = pl.BlockSpec((attn.shape[0], tm, PAIR), lambda i: (0, i, 0)) if attn_slabs else row(attn.shape[1])
    params = [prm[k] for k in ("ln_x_w", "ln_x_b", "attn_gain", "w_out", "norm_ffn", "wr_hi", "wr_lo", "b_router", "bd")]
    return pl.pallas_call(
        functools.partial(_post_kernel, attn_slabs=attn_slabs, rw=rw),
        grid=(n // tm,),
        in_specs=[row(d), row(rw), row(rw), row(rw), at_spec] + [_full(p.shape) for p in params],
        out_specs=[row(d), row(d), row(ROUTER_LANES), row(ROUTER_LANES)],
        out_shape=[jax.ShapeDtypeStruct((n, d), F32), jax.ShapeDtypeStruct((n, d), BF16)]
                  + [jax.ShapeDtypeStruct((n, ROUTER_LANES), F32)] * 2,
        compiler_params=_cparams(1),
        name="post",
    )(x, y, g, bonus, attn, *params)


MOE_ROW_BLOCK = 128
MOE_VMEM_LIMIT = 62 * 1024 * 1024
MOE_EXPERTS_PER_STEP = 4
MOE_SEG_ALIGN = 16
TOP_K = 2


def _moe_rows(tm):
    rows = TOP_K * tm + N_EXPERTS * (MOE_SEG_ALIGN - 1)
    return -(-rows // LANES) * LANES


def _moe_kernel(hn_ref, comb_ref, sel_ref, tri_ref, wg_ref, wu_ref, wd_ref, o_ref, p_all, xs, ys, cs, meta):
    e = pl.program_id(1)
    tm = hn_ref.shape[0]
    rows = p_all.shape[1]
    tn = lambda a_, b_: lax.dot_general(a_, b_, (((0,), (0,)), ((), ())), preferred_element_type=F32)

    @pl.when(e == 0)
    def _():
        sel = sel_ref[...]
        picked = sel > 0.0
        rank = jnp.dot(tri_ref[...], sel.astype(BF16), preferred_element_type=F32)
        cnt = rank[tm - 1:tm] + sel[tm - 1:tm]
        seg = jnp.ceil(cnt * (1.0 / MOE_SEG_ALIGN))
        li = lax.broadcasted_iota(jnp.int32, (LANES, LANES), 0)
        lj = lax.broadcasted_iota(jnp.int32, (LANES, LANES), 1)
        before = (li < lj).astype(BF16)
        off = jnp.dot(jnp.broadcast_to(seg, (8, LANES)).astype(BF16), before,
                      preferred_element_type=F32)[0:1] * float(MOE_SEG_ALIGN)
        meta[0:1, :] = cnt
        meta[1:2, :] = off
        dest = off + rank
        d_lo = jnp.min(jnp.where(picked, dest, 1e9), axis=-1, keepdims=True)
        d_hi = jnp.max(jnp.where(picked, dest, -1.0), axis=-1, keepdims=True)
        comb = comb_ref[...]
        w_lo = jnp.sum(jnp.where(picked & (dest == d_lo), comb, 0.0), axis=-1, keepdims=True)
        w_hi = jnp.sum(jnp.where(picked & (dest == d_hi), comb, 0.0), axis=-1, keepdims=True)
        d_lo, d_hi = d_lo.astype(jnp.int32), d_hi.astype(jnp.int32)
        hn = hn_ref[...]
        chunk = next(c for c in (256, 128) if rows % c == 0)
        for c0 in range(0, rows, chunk):
            row_id = c0 + lax.broadcasted_iota(jnp.int32, (tm, chunk), 1)
            is_lo = row_id == d_lo
            is_hi = row_id == d_hi
            one = jnp.where(is_lo | is_hi, 1.0, 0.0).astype(BF16)
            p_all[:, c0:c0 + chunk] = one
            xs[c0:c0 + chunk, :] = tn(one, hn).astype(BF16)
            crow = jnp.sum(jnp.where(is_lo, w_lo, 0.0) + jnp.where(is_hi, w_hi, 0.0), axis=0, keepdims=True)
            for k0 in range(0, chunk, LANES):
                diag = jnp.where(li == lj, jnp.broadcast_to(crow[:, k0:k0 + LANES], (LANES, LANES)), 0.0)
                cs[c0 + k0:c0 + k0 + LANES, :] = jnp.broadcast_to(jnp.sum(diag, axis=1, keepdims=True), (LANES, LANES))
        xs[rows:, :] = jnp.zeros((xs.shape[0] - rows, xs.shape[1]), BF16)
        ys[...] = jnp.zeros_like(ys)
        cs[rows:, :] = jnp.zeros((cs.shape[0] - rows, LANES), F32)

    lane = lax.broadcasted_iota(jnp.int32, (1, LANES), 1)
    n_here = wg_ref.shape[0]
    counts, offs, n_blocks = [], [], []
    for j in range(n_here):
        mine = lane == e * n_here + j + EXPERT_LANE0
        counts.append(jnp.sum(jnp.where(mine, meta[0:1, :], 0.0)).astype(jnp.int32))
        offs.append(jnp.sum(jnp.where(mine, meta[1:2, :], 0.0)).astype(jnp.int32))
        n_blocks.append((counts[j] + MOE_ROW_BLOCK - 1) // MOE_ROW_BLOCK)
    row_in_block = lax.broadcasted_iota(jnp.int32, (MOE_ROW_BLOCK, 1), 0)

    def expert_block(j, b):
        bj = jnp.minimum(b, jnp.maximum(n_blocks[j] - 1, 0))
        r = pl.ds(pl.multiple_of(offs[j] + bj * MOE_ROW_BLOCK, MOE_SEG_ALIGN), MOE_ROW_BLOCK)
        x = xs[r, :]
        gate = jnp.dot(x, wg_ref[j], preferred_element_type=F32)
        up = jnp.dot(x, wu_ref[j], preferred_element_type=F32)
        yield
        act = (gate * jax.nn.sigmoid(gate) * up * cs[r, 0:1]).astype(BF16)
        y = jnp.dot(act, wd_ref[j], preferred_element_type=F32).astype(BF16)
        yield
        ys[r, :] = jnp.where(row_in_block < counts[j] - bj * MOE_ROW_BLOCK, y, ys[r, :])

    def blocks(b, carry):
        _lockstep([expert_block(j, b) for j in range(n_here)])
        return carry

    lax.fori_loop(0, functools.reduce(jnp.maximum, n_blocks), blocks, 0)

    @pl.when(e == pl.num_programs(1) - 1)
    def _():
        o_ref[...] = jnp.dot(p_all[...], ys[:rows, :], preferred_element_type=F32)


def _moe(hn, comb, sel, wg, wu, wd, tm):
    n, d = hn.shape
    n_exp, _, de = wg.shape
    rows = _moe_rows(tm)
    tri = jnp.asarray(np.tril(np.ones((tm, tm), np.float32), -1), BF16)
    tile = lambda w: pl.BlockSpec((tm, w), lambda i, e: (i, 0))
    return pl.pallas_call(
        _moe_kernel,
        grid=(n // tm, n_exp // MOE_EXPERTS_PER_STEP),
        in_specs=[tile(d), tile(ROUTER_LANES), tile(ROUTER_LANES),
                  pl.BlockSpec((tm, tm), lambda i, e: (0, 0)),
                  pl.BlockSpec((MOE_EXPERTS_PER_STEP, d, de), lambda i, e: (e, 0, 0)),
                  pl.BlockSpec((MOE_EXPERTS_PER_STEP, d, de), lambda i, e: (e, 0, 0)),
                  pl.BlockSpec((MOE_EXPERTS_PER_STEP, de, d), lambda i, e: (e, 0, 0))],
        out_specs=tile(d),
        out_shape=jax.ShapeDtypeStruct((n, d), F32),
        scratch_shapes=[pltpu.VMEM((tm, rows), BF16),
                        pltpu.VMEM((rows + MOE_ROW_BLOCK, d), BF16), pltpu.VMEM((rows + MOE_ROW_BLOCK, d), BF16),
                        pltpu.VMEM((rows + MOE_ROW_BLOCK, LANES), F32), pltpu.VMEM((8, LANES), F32)],
        compiler_params=_cparams(2, MOE_VMEM_LIMIT),
        name="moe",
    )(hn, comb, sel, tri, wg, wu, wd)


def _tail_kernel(x1_ref, moe_ref, p_ref, nple_ref, wgate_ref, wple_ref, nfin_ref, y_ref):
    x2 = x1_ref[...] + moe_ref[...]
    gate = jax.nn.sigmoid(jnp.dot(_rms(x2, nple_ref[...]).astype(BF16), wgate_ref[...], preferred_element_type=F32))
    x3 = x2 + jnp.dot(p_ref[...].astype(BF16), wple_ref[...], preferred_element_type=F32) * gate
    y_ref[...] = _rms(x3, nfin_ref[...])


def _tail(x1, moe, p, prm, tm):
    n, d = x1.shape
    row = lambda w: pl.BlockSpec((tm, w), lambda i: (i, 0))
    params = [prm[k] for k in ("norm_ple", "w_ple_gate", "w_ple", "norm_final")]
    return pl.pallas_call(
        _tail_kernel,
        grid=(n // tm,),
        in_specs=[row(d), row(d), row(p.shape[1])] + [_full(a.shape) for a in params],
        out_specs=row(d),
        out_shape=jax.ShapeDtypeStruct((n, d), F32),
        compiler_params=_cparams(1),
        name="tail",
    )(x1, moe, p, *params)


def _layer_params(i, norm_mix, w_in, mu, w0, w2, a0, a2, g2, k_k, k_a, r_k, ln_x_w, ln_x_b, attn_gain, w_out,
                  norm_ffn, w_group, b_group, w_expert_router, b_expert_router, w_gate, w_up, w_down,
                  norm_ple, w_ple, w_ple_gate, norm_final):
    rw = w0.shape[1]
    d = w_in.shape[1]
    vec = lambda a: a[i].reshape(1, -1)
    lora_rows = DECAY_LORA + AAA_LORA + GATE_LORA

    def lora(w, lo):
        return jnp.zeros((lora_rows, rw), F32).at[lo:lo + w.shape[0]].set(w).astype(BF16)

    head = np.arange(rw) // HEAD_DIM
    w_router = jnp.zeros((d, ROUTER_LANES), F32)
    w_router = w_router.at[:, :N_GROUPS].set(w_group[i]).at[:, EXPERT_LANE0:EXPERT_LANE0 + N_EXPERTS].set(w_expert_router[i])
    wr_hi, wr_lo = _split2(w_router)
    b_router = jnp.zeros((1, ROUTER_LANES), F32)
    b_router = b_router.at[0, :N_GROUPS].set(b_group[i]).at[0, EXPERT_LANE0:EXPERT_LANE0 + N_EXPERTS].set(b_expert_router[i])
    return dict(
        norm_mix=vec(norm_mix), w_in=w_in[i].astype(BF16), mu=vec(mu), w0=vec(w0), a0=vec(a0),
        w_kv_t=w_in[i][:, w_in.shape[2] - 2 * ((w_in.shape[2] - mu.shape[1]) // 3):].T.astype(BF16),
        w2p=lora(w2[i], 0), a2p=lora(a2[i], DECAY_LORA), g2p=lora(g2[i], DECAY_LORA + AAA_LORA),
        k_k=vec(k_k), k_a=vec(k_a), r_k=vec(r_k), ln_x_w=vec(ln_x_w), ln_x_b=vec(ln_x_b),
        bd=jnp.asarray(head[:, None] == head[None, :], BF16),
        attn_gain=vec(attn_gain), w_out=w_out[i].astype(BF16), norm_ffn=vec(norm_ffn),
        wr_hi=wr_hi, wr_lo=wr_lo, b_router=b_router,
        w_gate=w_gate[i].astype(BF16), w_up=w_up[i].astype(BF16), w_down=w_down[i].astype(BF16),
        norm_ple=vec(norm_ple), w_ple=w_ple[i].astype(BF16), w_ple_gate=w_ple_gate[i].astype(BF16),
        norm_final=norm_final.reshape(1, -1),
    )


def _tile(n, want):
    t = min(n, want)
    assert n % t == 0, (n, t)
    return t


def _ffn_and_tail(x, y, g, bonus, attn, p, prm, rw):
    n = x.shape[0]
    x1, hn, comb, sel = _post(x, y, g, bonus, attn, prm, rw, _tile(n, 512))
    moe = _moe(hn, comb, sel, prm["w_gate"], prm["w_up"], prm["w_down"], _tile(n, 1024))
    return _tail(x1, moe, p, prm, _tile(n, 512))


def kernel(x_prompt, x_sample, state_wkv, state_shift, cache_k_win, cache_v_win, p_prompt, p_sample, norm_mix, w_in, mu, w0, w2, a0, a2, g2, k_k, k_a, r_k, ln_x_w, ln_x_b, attn_gain, w_out, norm_ffn, w_group, b_group, w_expert_router, b_expert_router, w_gate, w_up, w_down, norm_ple, w_ple, w_ple_gate, norm_final):
    depth = norm_mix.shape[0]
    assert depth == 1, "single-layer step"
    bp, s_len, d = x_prompt.shape
    bs, t_len, _ = x_sample.shape
    rw = w0.shape[1]
    n_rw_heads = rw // HEAD_DIM
    rw_cols = mu.shape[1]
    at_width = (w_in.shape[2] - rw_cols) // 3
    n_at_heads = at_width // HEAD_DIM
    win = cache_k_win.shape[2]
    assert s_len % (max(DILATIONS) * ATT_BLOCK) == 0 and s_len >= win and t_len == 8 and bs % 8 == 0
    assert all(wd // dl == ATT_BLOCK for wd, dl in zip(WINDOWS, DILATIONS)) and win >= max(WINDOWS)

    prm = _layer_params(0, norm_mix, w_in, mu, w0, w2, a0, a2, g2, k_k, k_a, r_k, ln_x_w, ln_x_b, attn_gain, w_out,
                        norm_ffn, w_group, b_group, w_expert_router, b_expert_router, w_gate, w_up, w_down,
                        norm_ple, w_ple, w_ple_gate, norm_final)

    n_p = bp * s_len
    xp = x_prompt.reshape(n_p, d)
    pr, q, k, v = _in_proj(xp, prm["norm_mix"], prm["w_in"], rw_cols, at_width, _tile(n_p, 512), True)
    tm = _tile(s_len, 512)
    r_, lw, km, vv, kk, kka, g, bonus = _rwkv_prep(pr, pr, prm, rw, tm, 1, s_len // tm)
    ct = _tile(s_len, 512)
    y, wkv_p = _wkv_prompt(bp, n_rw_heads, (r_, lw, km, vv, kk, kka), ct)
    attn = _attn_prompt(q, k, v, bp, n_at_heads)
    y_prompt = _ffn_and_tail(xp, y, g, bonus, attn, p_prompt[0].reshape(n_p, -1), prm, rw).reshape(bp, s_len, d)
    shift_p = pr.reshape(bp, s_len, rw_cols)[:, -1]
    keep = min(win, s_len)
    kv_t = _win_proj(x_prompt, prm["norm_mix"], prm["w_kv_t"], keep, _tile(keep, 512))
    window = lambda z: jnp.transpose(z.reshape(bp, n_at_heads, HEAD_DIM, keep), (0, 3, 1, 2))
    kwin_p, vwin_p = window(kv_t[:, :at_width]), window(kv_t[:, at_width:])

    n_s = bs * t_len
    xs = jnp.swapaxes(x_sample, 0, 1).reshape(n_s, d)
    ps = jnp.swapaxes(p_sample[0], 0, 1).reshape(n_s, -1)
    pr_s, q_s, k_s, v_s = _in_proj(xs, prm["norm_mix"], prm["w_in"], rw_cols, at_width, _tile(n_s, 256), False)
    r_, lw, km, vv, kk, kka, g, bonus = _rwkv_prep(pr_s, state_shift[0], prm, rw, n_s, bs, 1)
    y, wkv_s = _wkv_sample(t_len, bs, n_rw_heads, (r_, lw, km, vv, kk, kka), state_wkv[0], 8)
    to_t = lambda c: jnp.transpose(c[0], (0, 2, 3, 1)).reshape(bs, at_width, win)
    from_t = lambda c: jnp.transpose(c.reshape(bs, n_at_heads, HEAD_DIM, win), (0, 3, 1, 2))[None]
    at_s, kwin_s, vwin_s = _attn_sample(q_s, k_s, v_s, to_t(cache_k_win), to_t(cache_v_win), t_len, bs, n_at_heads)
    y_s = _ffn_and_tail(xs, y, g, bonus, at_s, ps, prm, rw)
    y_sample = jnp.swapaxes(y_s.reshape(t_len, bs, d), 0, 1)
    shift_s = pr_s[(t_len - 1) * bs:]

    return (y_prompt, y_sample, wkv_p[None], shift_p[None], kwin_p[None], vwin_p[None],
            wkv_s[None], shift_s[None], from_t(kwin_s), from_t(vwin_s))
```

```python
import functools

import numpy as np
import jax
import jax.numpy as jnp
from jax import lax
from jax.experimental import pallas as pl
from jax.experimental.pallas import tpu as pltpu

F32 = jnp.float32
BF16 = jnp.bfloat16

LANES = 128
HEAD_DIM = 64
PAIR = 2 * HEAD_DIM
DECAY_LORA = 32
AAA_LORA = 32
GATE_LORA = 64
GN_EPS = 64e-5
RMS_EPS = 1e-6
WINDOWS = (128, 512, 2048)
DILATIONS = (1, 4, 16)
ATT_BLOCK = 128
N_GROUPS = 4
EXPERTS_PER_GROUP = 8
N_EXPERTS = N_GROUPS * EXPERTS_PER_GROUP
ROUTER_LANES = 128
EXPERT_LANE0 = N_GROUPS
WKV_CHUNK = 64
VMEM_LIMIT = 56 * 1024 * 1024


def _cparams(n_axes, vmem_limit=VMEM_LIMIT):
    return pltpu.CompilerParams(dimension_semantics=("arbitrary",) * n_axes, vmem_limit_bytes=vmem_limit)


def _full(shape):
    nd = len(shape)
    return pl.BlockSpec(shape, lambda *_: (0,) * nd)


def _bdot(a, b):
    return jnp.dot(a.astype(BF16), b.astype(BF16), preferred_element_type=F32)


def _split2(a):
    hi = a.astype(BF16)
    lo = (a - hi.astype(F32)).astype(BF16)
    return hi, lo


def _hilo_dot(a, b_bf16):
    hi, lo = _split2(a)
    return (jnp.dot(hi, b_bf16, preferred_element_type=F32)
            + jnp.dot(lo, b_bf16, preferred_element_type=F32))


def _rms(x, g):
    return x * lax.rsqrt(jnp.mean(x * x, axis=-1, keepdims=True) + RMS_EPS) * g


def _win_proj_kernel(x_ref, g_ref, wt_ref, o_ref):
    h = _rms(x_ref[0], g_ref[...]).astype(BF16)
    o_ref[0] = lax.dot_general(wt_ref[...], h, (((1,), (1,)), ((), ())), preferred_element_type=F32)


def _win_proj(x, g, wt_bf16, keep, tm):
    seqs, s_len, d = x.shape
    cols = wt_bf16.shape[0]
    first = (s_len - keep) // tm
    return pl.pallas_call(
        _win_proj_kernel,
        grid=(seqs, keep // tm),
        in_specs=[pl.BlockSpec((1, tm, d), lambda b, j: (b, first + j, 0)), _full(g.shape), _full(wt_bf16.shape)],
        out_specs=pl.BlockSpec((1, cols, tm), lambda b, j: (b, 0, j)),
        out_shape=jax.ShapeDtypeStruct((seqs, cols, keep), F32),
        compiler_params=_cparams(2),
        name="win_proj",
    )(x, g, wt_bf16)


def _proj_prep_kernel(x_ref, gn_ref, w_ref, bnd_ref, mu_ref, w0_ref, w2_ref, a0_ref, a2_ref, g2_ref, kk_ref, ka_ref,
                      rk_ref, bd_ref, q_ref, k_ref, v_ref, r_o, lw_o, k_o, v_o, kk_o, kka_o, g_o, bonus_o, shift_o,
                      carry, *, rw_cols, at_width, slabs, rw, time_major, tiles_per_seq):
    j = pl.program_id(0) % tiles_per_seq

    @pl.when(j == 0)
    def _():
        carry[...] = bnd_ref[...]

    h = _rms(x_ref[...], gn_ref[...]).astype(BF16)
    pr = jnp.dot(h, w_ref[:, :rw_cols], preferred_element_type=F32)
    tm = pr.shape[0]
    if time_major:
        prev = carry[...]
        carry[...] = pr
        shift_o[...] = pr
    else:
        row = lax.broadcasted_iota(jnp.int32, pr.shape, 0)
        prev = jnp.where(row == 0, carry[0:1, :], pltpu.roll(pr, 1, 0))
        carry[0:1, :] = pr[tm - 1:tm, :]
        shift_o[0] = pr[tm - 1:tm, :]

    for i, o_ref in enumerate((q_ref, k_ref, v_ref)):
        lo = rw_cols + i * at_width
        z = jnp.dot(h, w_ref[:, lo:lo + at_width], preferred_element_type=F32)
        if slabs:
            for s_ in range(at_width // PAIR):
                o_ref[s_] = z[:, s_ * PAIR:(s_ + 1) * PAIR]
        else:
            o_ref[...] = z

    xm = pr + (prev - pr) * mu_ref[...]
    xr, xk, xv, xl = xm[:, :rw], xm[:, rw:2 * rw], xm[:, 2 * rw:3 * rw], xm[:, 3 * rw:]
    bd = bd_ref[...]
    z = -(w0_ref[...] + _bdot(jnp.tanh(xl), w2_ref[...]))
    softplus = jnp.maximum(z, 0.0) + jnp.log(1.0 + jnp.exp(-jnp.abs(z)))
    lw = -jnp.exp(-softplus - 0.5)
    a = jax.nn.sigmoid(a0_ref[...] + _bdot(xl, a2_ref[...]))
    g = _bdot(jax.nn.sigmoid(xl), g2_ref[...])
    kk = xk * kk_ref[...]
    kk = kk / jnp.maximum(jnp.sqrt(_hilo_dot(kk * kk, bd)), 1e-12)
    kmod = xk * (1.0 + (a - 1.0) * ka_ref[...])
    bonus = _hilo_dot(xr * kmod * rk_ref[...], bd) * xv
    r_o[...] = xr
    lw_o[...] = lw
    k_o[...] = kmod
    v_o[...] = xv
    kk_o[...] = kk
    kka_o[...] = kk * a
    g_o[...] = g
    bonus_o[...] = bonus


def _proj_prep(x, bnd, prm, rw_cols, at_width, rw, tm, slabs, time_major, tiles_per_seq):
    n, d = x.shape
    row = lambda w: pl.BlockSpec((tm, w), lambda i: (i, 0))
    if slabs:
        qkv_spec = pl.BlockSpec((at_width // PAIR, tm, PAIR), lambda i: (0, i, 0))
        qkv_shape = jax.ShapeDtypeStruct((at_width // PAIR, n, PAIR), F32)
    else:
        qkv_spec, qkv_shape = row(at_width), jax.ShapeDtypeStruct((n, at_width), F32)
    if time_major:
        shift_spec, shift_shape = _full((tm, rw_cols)), jax.ShapeDtypeStruct((tm, rw_cols), F32)
        carry_rows = tm
    else:
        n_seq = n // tm // tiles_per_seq
        shift_spec = pl.BlockSpec((1, 1, rw_cols), lambda i: (i // tiles_per_seq, 0, 0))
        shift_shape = jax.ShapeDtypeStruct((n_seq, 1, rw_cols), F32)
        carry_rows = 8
    params = [prm[k] for k in ("mu", "w0", "w2p", "a0", "a2p", "g2p", "k_k", "k_a", "r_k", "bd")]
    return pl.pallas_call(
        functools.partial(_proj_prep_kernel, rw_cols=rw_cols, at_width=at_width, slabs=slabs, rw=rw,
                          time_major=time_major, tiles_per_seq=tiles_per_seq),
        grid=(n // tm,),
        in_specs=[row(d), _full(prm["norm_mix"].shape), _full(prm["w_in"].shape), _full(bnd.shape)]
                 + [_full(p.shape) for p in params],
        out_specs=[qkv_spec] * 3 + [row(rw)] * 8 + [shift_spec],
        out_shape=[qkv_shape] * 3 + [jax.ShapeDtypeStruct((n, rw), F32)] * 8 + [shift_shape],
        scratch_shapes=[pltpu.VMEM((carry_rows, rw_cols), F32)],
        compiler_params=_cparams(1),
        name="proj_prep",
    )(x, prm["norm_mix"], prm["w_in"], bnd, *params)


def _cumsum_rows(x):
    n = x.shape[0]
    row = lax.broadcasted_iota(jnp.int32, x.shape, 0)
    s = 1
    while s < n:
        x = x + jnp.where(row >= s, pltpu.roll(x, s, 0), 0.0)
        s *= 2
    return x


def _lockstep(gens):
    results = [None] * len(gens)
    live = list(range(len(gens)))
    while live:
        still = []
        for i in live:
            try:
                next(gens[i])
                still.append(i)
            except StopIteration as stop:
                results[i] = stop.value
        live = still
    return results


def _wkv_head_chunk(pt, rt, kt, qt, kt_end, qt_end, vh, p_end, s_h, masks):
    strict, incl, eye = masks
    c = pt.shape[0]
    nt = lambda a_, b_: lax.dot_general(a_, b_, (((1,), (1,)), ((), ())), preferred_element_type=F32)
    nn = lambda a_, b_: jnp.dot(a_, b_, preferred_element_type=F32)
    tn = lambda a_, b_: lax.dot_general(a_, b_, (((0,), (0,)), ((), ())), preferred_element_type=F32)
    lhs = jnp.concatenate([pt, rt], axis=0)
    gq = nt(lhs, qt)
    gk = nt(lhs, kt)
    yield
    l_qp = jnp.where(strict, gq[:c], 0.0)
    l_kp = jnp.where(strict, gk[:c], 0.0)
    a_rq = jnp.where(incl, gq[c:], 0.0)
    a_rk = jnp.where(incl, gk[c:], 0.0)
    t_inv = eye - l_qp
    pw = -l_qp
    lv_av = nn(jnp.concatenate([l_kp, a_rk], axis=0), vh)
    for _ in range(int(np.log2(c)) - 1):
        pw = nn(pw, pw)
        yield
        t_new = nn(pw, t_inv)
        yield
        t_inv = t_inv + t_new
    w_mat = nn(t_inv, pt)
    u_v = nn(t_inv, lv_av[:c])
    yield
    ws = nt(jnp.concatenate([w_mat, rt], axis=0), s_h)
    yield
    u = u_v + ws[:c]
    au = nn(a_rq, u)
    upd = tn(jnp.concatenate([vh, -u], axis=0), jnp.concatenate([kt_end, qt_end], axis=0))
    yield
    return ws[c:] + lv_av[c:] - au, s_h * p_end + upd


def _pair_stack(z, first):
    return jnp.concatenate([jnp.where(first, z, 0.0), jnp.where(first, 0.0, z)], axis=0).astype(BF16)


def _wkv_pair_static(slabs, masks):
    r, lw, k, v, kk, kka = slabs
    first, strict, incl, eye = masks
    c = r.shape[0]
    n2 = 2 * c
    nt = lambda a_, b_: lax.dot_general(a_, b_, (((1,), (1,)), ((), ())), preferred_element_type=F32)
    nn = lambda a_, b_: jnp.dot(a_, b_, preferred_element_type=F32)
    cum = _cumsum_rows(lw)
    cum_last = cum[c - 1:c, :]
    d_end = jnp.exp(cum_last - cum)
    p_inv = jnp.exp(-cum)
    ps = _pair_stack(kk * jnp.exp(cum - lw), first)
    rs = _pair_stack(r * jnp.exp(cum), first)
    qs = _pair_stack(kka * p_inv, first)
    ks = _pair_stack(k * p_inv, first)
    vs = _pair_stack(v, first)
    ends = jnp.concatenate([_pair_stack(k * d_end, first), _pair_stack(kka * d_end, first)], axis=0)
    g = nt(jnp.concatenate([ps, rs], axis=0), jnp.concatenate([qs, ks], axis=0))
    yield
    l_qp = jnp.where(strict, g[:n2, :n2], 0.0)
    l_kp = jnp.where(strict, g[:n2, n2:], 0.0).astype(BF16)
    a_rq = jnp.where(incl, g[n2:, :n2], 0.0).astype(BF16)
    a_rk = jnp.where(incl, g[n2:, n2:], 0.0).astype(BF16)
    t_inv = eye - l_qp
    pw = (-l_qp).astype(BF16)
    lv_av = nn(jnp.concatenate([l_kp, a_rk], axis=0), vs)
    pw2 = nn(pw, pw)
    yield
    lv, av = lv_av[:n2].astype(BF16), lv_av[n2:]
    pw = pw2.astype(BF16)
    n_it = int(np.log2(c)) - 1
    for i in range(n_it):
        if i < n_it - 1:
            out = nn(pw, jnp.concatenate([pw, t_inv.astype(BF16)], axis=1))
            yield
            pw = out[:, :n2].astype(BF16)
            t_inv = t_inv + out[:, n2:]
        else:
            out = nn(pw, t_inv.astype(BF16))
            yield
            t_inv = t_inv + out
    tp = nn(t_inv.astype(BF16), jnp.concatenate([ps, lv], axis=1))
    yield
    return dict(w=tp[:, :PAIR].astype(BF16), u_v=tp[:, PAIR:], rs=rs, vs=vs, a_rq=a_rq, av=av, ends=ends,
                p_end=jnp.exp(cum_last))


def _wkv_pair_step(st, s_bd):
    nt = lambda a_, b_: lax.dot_general(a_, b_, (((1,), (1,)), ((), ())), preferred_element_type=F32)
    tn = lambda a_, b_: lax.dot_general(a_, b_, (((0,), (0,)), ((), ())), preferred_element_type=F32)
    n2 = st["w"].shape[0]
    ws = nt(jnp.concatenate([st["w"], st["rs"]], axis=0), s_bd.astype(BF16))
    yield
    u = st["u_v"] + ws[:n2]
    upd = tn(jnp.concatenate([st["vs"], (-u).astype(BF16)], axis=0), st["ends"])
    au = jnp.dot(st["a_rq"], u.astype(BF16), preferred_element_type=F32)
    yield
    y_bd = ws[n2:] + st["av"] - au
    return y_bd[:n2 // 2] + y_bd[n2 // 2:], s_bd * st["p_end"] + upd


def _wkv_prompt_kernel(r_ref, lw_ref, k_ref, v_ref, kk_ref, kka_ref, y_ref, sT_ref, s_scr, *, chunks, tiles_per_seq):
    j = pl.program_id(0) % tiles_per_seq
    n_pairs = s_scr.shape[0]
    c = WKV_CHUNK

    @pl.when(j == 0)
    def _():
        s_scr[...] = jnp.zeros_like(s_scr)

    lane = lax.broadcasted_iota(jnp.int32, (c, PAIR), 1)
    ri = lax.broadcasted_iota(jnp.int32, (2 * c, 2 * c), 0)
    ci = lax.broadcasted_iota(jnp.int32, (2 * c, 2 * c), 1)
    rt_, ct_ = ri % c, ci % c
    masks = (lane < HEAD_DIM, rt_ > ct_, rt_ >= ct_, (ri == ci).astype(F32))

    gens = []
    for ch in range(chunks):
        rows = slice(ch * c, (ch + 1) * c)
        for p in range(n_pairs):
            lanes = slice(p * PAIR, (p + 1) * PAIR)
            slabs = [ref[rows, lanes] for ref in (r_ref, lw_ref, k_ref, v_ref, kk_ref, kka_ref)]
            gens.append(_wkv_pair_static(slabs, masks))
    static = _lockstep(gens)
    states = [s_scr[p] for p in range(n_pairs)]
    for ch in range(chunks):
        outs = _lockstep([_wkv_pair_step(static[ch * n_pairs + p], states[p]) for p in range(n_pairs)])
        for p, (y, s_new) in enumerate(outs):
            states[p] = s_new
            y_ref[ch * c:(ch + 1) * c, p * PAIR:(p + 1) * PAIR] = y
    for p in range(n_pairs):
        s_scr[p] = states[p]

    @pl.when(j == tiles_per_seq - 1)
    def _():
        for p in range(n_pairs):
            sT_ref[0, 2 * p] = s_scr[p, :HEAD_DIM, :HEAD_DIM]
            sT_ref[0, 2 * p + 1] = s_scr[p, HEAD_DIM:, HEAD_DIM:]


def _wkv_prompt(seqs, n_heads, arrays, ct):
    n, w = arrays[0].shape
    tiles_per_seq = n // seqs // ct
    row = pl.BlockSpec((ct, w), lambda i: (i, 0))
    st_shape = (seqs, n_heads, HEAD_DIM, HEAD_DIM)
    return pl.pallas_call(
        functools.partial(_wkv_prompt_kernel, chunks=ct // WKV_CHUNK, tiles_per_seq=tiles_per_seq),
        grid=(n // ct,),
        in_specs=[row] * 6,
        out_specs=[row, pl.BlockSpec((1,) + st_shape[1:], lambda i: (i // tiles_per_seq, 0, 0, 0))],
        out_shape=[jax.ShapeDtypeStruct((n, w), F32), jax.ShapeDtypeStruct(st_shape, F32)],
        scratch_shapes=[pltpu.VMEM((n_heads // 2, PAIR, PAIR), F32)],
        compiler_params=_cparams(1),
        name="wkv_prompt",
    )(*arrays)


def _wkv_sample_kernel(r_ref, lw_ref, k_ref, v_ref, kk_ref, kka_ref, s0_ref, y_ref, sT_ref, *, seqs_per_step):
    n_heads = s0_ref.shape[1]
    c = r_ref.shape[0]
    ri = lax.broadcasted_iota(jnp.int32, (c, c), 0)
    ci = lax.broadcasted_iota(jnp.int32, (c, c), 1)
    masks = (ri > ci, ri >= ci, (ri == ci).astype(F32))
    gens = []
    for s in range(seqs_per_step):
        r, lw, k, v, kk, kka = [ref[:, s, 0, :] for ref in (r_ref, lw_ref, k_ref, v_ref, kk_ref, kka_ref)]
        cum = _cumsum_rows(lw)
        cum_last = cum[c - 1:c, :]
        p_inv = jnp.exp(-cum)
        d_end = jnp.exp(cum_last - cum)
        ops = (kk * jnp.exp(cum - lw), r * jnp.exp(cum), k * p_inv, kka * p_inv, k * d_end, kka * d_end, v,
               jnp.exp(cum_last))
        for h in range(n_heads):
            hs = slice(h * HEAD_DIM, (h + 1) * HEAD_DIM)
            gens.append(_wkv_head_chunk(*[z[:, hs] for z in ops], s0_ref[s, h], masks))
    outs = _lockstep(gens)
    for s in range(seqs_per_step):
        heads = outs[s * n_heads:(s + 1) * n_heads]
        y_ref[:, s, 0, :] = jnp.concatenate([y for y, _ in heads], axis=1)
        for h, (_, s_new) in enumerate(heads):
            sT_ref[s, h] = s_new


def _wkv_sample(t_len, seqs, n_heads, arrays, s0, seqs_per_step):
    w = arrays[0].shape[-1]
    arrays = [a.reshape(t_len, seqs, 1, w) for a in arrays]
    tok = pl.BlockSpec((t_len, seqs_per_step, 1, w), lambda i: (0, i, 0, 0))
    st = pl.BlockSpec((seqs_per_step, n_heads, HEAD_DIM, HEAD_DIM), lambda i: (i, 0, 0, 0))
    y, s_t = pl.pallas_call(
        functools.partial(_wkv_sample_kernel, seqs_per_step=seqs_per_step),
        grid=(seqs // seqs_per_step,),
        in_specs=[tok] * 6 + [st],
        out_specs=[tok, st],
        out_shape=[jax.ShapeDtypeStruct((t_len, seqs, 1, w), F32), jax.ShapeDtypeStruct(s0.shape, F32)],
        compiler_params=_cparams(1),
        name="wkv_sample",
    )(*arrays, s0)
    return y.reshape(t_len * seqs, w), s_t


ATT_SPAN = max(DILATIONS) * ATT_BLOCK


def _attn_prompt_kernel(q_ref, kp_ref, kc_ref, vp_ref, vc_ref, o_ref, kx, vx, ob, lb, bias_scr, *, n_heads,
                        tiles_per_iter):
    slab, blk = pl.program_id(1), pl.program_id(2)
    span = ATT_SPAN
    kx[:span] = kp_ref[0]
    kx[span:] = kc_ref[0]
    vx[:span] = vp_ref[0]
    vx[span:] = vc_ref[0]
    qi = lax.broadcasted_iota(jnp.int32, (ATT_BLOCK, 2 * ATT_BLOCK), 0)
    ki = lax.broadcasted_iota(jnp.int32, (ATT_BLOCK, 2 * ATT_BLOCK), 1)
    steps = ATT_BLOCK + qi - ki
    band = (steps >= 0) & (steps <= ATT_BLOCK)
    stepsf = steps.astype(F32)
    first = lax.broadcasted_iota(jnp.int32, (ATT_BLOCK, PAIR), 1) < HEAD_DIM
    log2e = float(np.log2(np.e))
    scale = HEAD_DIM ** -0.5 * log2e
    for h2 in range(2):
        alibi = jnp.exp2(jnp.zeros_like(stepsf) - (8.0 / n_heads) * (2 * slab + 1 + h2).astype(F32)) * (stepsf * log2e)
        for br, dil in enumerate(DILATIONS):
            bias_scr[0, br, h2] = jnp.where(band, alibi * float(dil), jnp.inf)
            bias_scr[1, br, h2] = jnp.where(band & (ki >= ATT_BLOCK), alibi * float(dil), jnp.inf)

    def head(q_h, k_t, v_t, bias):
        s = lax.dot_general(q_h, k_t, (((1,), (1,)), ((), ())), preferred_element_type=F32)
        yield
        s = s - bias
        m = jnp.max(s, axis=-1, keepdims=True)
        e = jnp.exp2(s - m)
        den = jnp.sum(e, axis=-1, keepdims=True)
        o = jnp.dot(e.astype(BF16), v_t, preferred_element_type=F32)
        yield
        return o / den, m + jnp.log2(den)

    def body(it, carry):
        gens, where_to = [], []
        for u in range(tiles_per_iter):
            t = it * tiles_per_iter + u
            for br, dil in enumerate(DILATIONS):
                sub = t // dil
                base = t % dil + dil * ATT_BLOCK * sub
                rows_q = pl.ds(base, ATT_BLOCK, stride=dil)
                rows_kv = pl.ds(span + base - dil * ATT_BLOCK, 2 * ATT_BLOCK, stride=dil)
                q_t = q_ref[0, rows_q, :] * scale
                k_t = kx[rows_kv, :].astype(BF16)
                v_t = vx[rows_kv, :].astype(BF16)
                at_start = jnp.where((blk > 0) | (sub > 0), 0, 1)
                for h2 in range(2):
                    q_h = jnp.where(first if h2 == 0 else ~first, q_t, 0.0).astype(BF16)
                    gens.append(head(q_h, k_t, v_t, bias_scr[at_start, br, h2]))
                where_to.append((br, rows_q))
        res = _lockstep(gens)
        for i, (br, rows_q) in enumerate(where_to):
            (o0, l0), (o1, l1) = res[2 * i], res[2 * i + 1]
            ob[br, rows_q, :] = jnp.where(first, o0, o1)
            lb[br, rows_q, :] = jnp.where(first, l0, l1)
        return carry

    lax.fori_loop(0, span // ATT_BLOCK // tiles_per_iter, body, 0)
    ls = [lb[br] for br in range(len(DILATIONS))]
    m = functools.reduce(jnp.maximum, ls)
    es = [jnp.exp2(l - m) for l in ls]
    o_ref[0] = sum(e * ob[br] for br, e in enumerate(es)) / sum(es)


def _attn_prompt(q, k, v, seqs, n_heads):
    n_slabs, n, _ = q.shape
    n_blk = n // seqs // ATT_SPAN
    cur = pl.BlockSpec((1, ATT_SPAN, PAIR), lambda b, s, i: (s, b * n_blk + i, 0))
    prev = pl.BlockSpec((1, ATT_SPAN, PAIR), lambda b, s, i: (s, b * n_blk + jnp.maximum(i - 1, 0), 0))
    return pl.pallas_call(
        functools.partial(_attn_prompt_kernel, n_heads=n_heads, tiles_per_iter=2),
        grid=(seqs, n_slabs, n_blk),
        in_specs=[cur, prev, cur, prev, cur],
        out_specs=cur,
        out_shape=jax.ShapeDtypeStruct(q.shape, F32),
        scratch_shapes=[pltpu.VMEM((2 * ATT_SPAN, PAIR), F32)] * 2
                       + [pltpu.VMEM((len(DILATIONS), ATT_SPAN, PAIR), F32)] * 2
                       + [pltpu.VMEM((2, len(DILATIONS), 2, ATT_BLOCK, 2 * ATT_BLOCK), F32)],
        compiler_params=_cparams(3),
        name="attn_prompt",
    )(q, k, k, v, v)


def _attn_sample_kernel(q_ref, kn_ref, vn_ref, kc_ref, vc_ref, o_ref, ko_ref, vo_ref, *, n_heads):
    t_len = q_ref.shape[0]
    w, win = kc_ref.shape[1], kc_ref.shape[2]
    rows = n_heads * t_len
    q = q_ref[:, 0, 0, :]
    kn = kn_ref[:, 0, 0, :]
    vn = vn_ref[:, 0, 0, :]
    nt = lambda a_, b_: lax.dot_general(a_, b_, (((1,), (1,)), ((), ())), preferred_element_type=F32)
    tn = lambda a_, b_: lax.dot_general(a_, b_, (((0,), (0,)), ((), ())), preferred_element_type=F32)

    pad_rows = 16
    place = (lax.broadcasted_iota(jnp.int32, (pad_rows, LANES), 1)
             == lax.broadcasted_iota(jnp.int32, (pad_rows, LANES), 0) + (LANES - t_len)).astype(BF16)
    lane = lax.broadcasted_iota(jnp.int32, (LANES, LANES), 1)

    def transposed_tail(z):
        zp = jnp.concatenate([z, jnp.zeros((pad_rows - t_len, w), F32)], axis=0)
        hi = zp.astype(BF16)
        r1 = zp - hi.astype(F32)
        mid = r1.astype(BF16)
        lo = (r1 - mid.astype(F32)).astype(BF16)
        return tn(hi, place) + tn(mid, place) + tn(lo, place)

    for src, new, dst in ((kc_ref, kn, ko_ref), (vc_ref, vn, vo_ref)):
        tail = transposed_tail(new)
        for r0 in range(0, w, LANES):
            rolled = pltpu.roll(src[0, r0:r0 + LANES, :], win - t_len, 1)
            dst[0, r0:r0 + LANES, :win - LANES] = rolled[:, :win - LANES]
            dst[0, r0:r0 + LANES, win - LANES:] = jnp.where(lane < LANES - t_len, rolled[:, win - LANES:],
                                                             tail[r0:r0 + LANES])

    qrep = jnp.concatenate([q] * n_heads, axis=0)
    rh = lax.broadcasted_iota(jnp.int32, (rows, w), 0) // t_len
    ch = lax.broadcasted_iota(jnp.int32, (rows, w), 1) // HEAD_DIM
    qexp = jnp.where(rh == ch, qrep, 0.0).astype(BF16)
    scale = HEAD_DIM ** -0.5

    def weights(n_keys, key_pos0):
        ri = lax.broadcasted_iota(jnp.int32, (rows, n_keys), 0)
        ki = lax.broadcasted_iota(jnp.int32, (rows, n_keys), 1)
        dist = win + ri % t_len - (key_pos0 + ki)
        cnt = jnp.zeros((rows, n_keys), F32)
        for wdw, dil in zip(WINDOWS, DILATIONS):
            ok = (dist >= 0) & (dist % dil == 0) & (dist <= wdw)
            cnt = cnt + jnp.where(ok, 1.0, 0.0)
        slope = jnp.exp2(-8.0 * (ri // t_len + 1).astype(F32) / n_heads)
        return cnt, slope * dist.astype(F32)

    cnt_c, bias_c = weights(win, 0)
    cnt_n, bias_n = weights(t_len, win)
    s_c = jnp.where(cnt_c > 0, jnp.dot(qexp, kc_ref[0].astype(BF16), preferred_element_type=F32) * scale - bias_c,
                    -jnp.inf)
    s_n = jnp.where(cnt_n > 0, nt(qexp, kn.astype(BF16)) * scale - bias_n, -jnp.inf)
    m = jnp.maximum(jnp.max(s_c, axis=-1, keepdims=True), jnp.max(s_n, axis=-1, keepdims=True))
    e_c = cnt_c * jnp.exp(s_c - m)
    e_n = cnt_n * jnp.exp(s_n - m)
    den = jnp.sum(e_c, axis=-1, keepdims=True) + jnp.sum(e_n, axis=-1, keepdims=True)
    acc = (nt(e_c.astype(BF16), vc_ref[0].astype(BF16)) + jnp.dot(e_n, vn, preferred_element_type=F32)) / den
    acc = jnp.where(rh == ch, acc, 0.0)
    out = acc[:t_len]
    for h in range(1, n_heads):
        out = out + acc[h * t_len:(h + 1) * t_len]
    o_ref[:, 0, 0, :] = out


def _attn_sample(q, k, v, kc, vc, t_len, seqs, n_heads):
    _, w, win = kc.shape
    tok = pl.BlockSpec((t_len, 1, 1, w), lambda b: (0, b, 0, 0))
    cache = pl.BlockSpec((1, w, win), lambda b: (b, 0, 0))
    v4 = lambda a: a.reshape(t_len, seqs, 1, w)
    o, ko, vo = pl.pallas_call(
        functools.partial(_attn_sample_kernel, n_heads=n_heads),
        grid=(seqs,),
        in_specs=[tok, tok, tok, cache, cache],
        out_specs=[tok, cache, cache],
        out_shape=[jax.ShapeDtypeStruct((t_len, seqs, 1, w), F32)] + [jax.ShapeDtypeStruct(kc.shape, F32)] * 2,
        compiler_params=_cparams(1),
        name="attn_sample",
    )(v4(q), v4(k), v4(v), kc, vc)
    return o.reshape(t_len * seqs, w), ko, vo


def _post_kernel(x_ref, y_ref, g_ref, bonus_ref, at_ref, lnw_ref, lnb_ref, gain_ref, wout_ref, nffn_ref, wr_hi_ref,
                 wr_lo_ref, br_ref, bd_ref, x1_ref, hn_ref, comb_ref, sel_ref, *, attn_slabs, rw):

    bd = bd_ref[...]
    y = y_ref[...]
    mean = _hilo_dot(y, bd) * (1.0 / HEAD_DIM)
    d = y - mean
    var = _hilo_dot(d * d, bd) * (1.0 / HEAD_DIM)
    yn = d * lax.rsqrt(var + GN_EPS) * lnw_ref[...] + lnb_ref[...]
    rw_out = (yn + bonus_ref[...]) * g_ref[...]

    if attn_slabs:
        at = jnp.concatenate([at_ref[s] for s in range(at_ref.shape[0])], axis=1)
    else:
        at = at_ref[...]
    at_out = _rms(at, gain_ref[...])

    x1 = (x_ref[...] + jnp.dot(rw_out.astype(BF16), wout_ref[:rw, :], preferred_element_type=F32)
          + jnp.dot(at_out.astype(BF16), wout_ref[rw:, :], preferred_element_type=F32))
    x1_ref[...] = x1
    hn = _rms(x1, nffn_ref[...])
    hn_ref[...] = hn.astype(BF16)

    hi, lo = _split2(hn)
    logits = (jnp.dot(hi, wr_hi_ref[...], preferred_element_type=F32)
              + jnp.dot(lo, wr_hi_ref[...], preferred_element_type=F32)
              + jnp.dot(hi, wr_lo_ref[...], preferred_element_type=F32)) + br_ref[...]
    lane = lax.broadcasted_iota(jnp.int32, logits.shape, 1)
    big = jnp.int32(ROUTER_LANES)
    first_max = lambda z: jnp.min(jnp.where(z == jnp.max(z, axis=-1, keepdims=True), lane, big), axis=-1, keepdims=True)
    gl = jnp.where(lane < N_GROUPS, logits, -jnp.inf)
    gsel = first_max(gl)
    gw = 1.0 / jnp.sum(jnp.exp(gl - jnp.max(gl, axis=-1, keepdims=True)), axis=-1, keepdims=True)
    e_lane = lane - EXPERT_LANE0
    in_group = (e_lane >= 0) & (e_lane < N_EXPERTS) & (e_lane // EXPERTS_PER_GROUP == gsel)
    el = jnp.where(in_group, logits, -jnp.inf)
    m1 = jnp.max(el, axis=-1, keepdims=True)
    i1 = first_max(el)
    el2 = jnp.where(lane == i1, -jnp.inf, el)
    m2 = jnp.max(el2, axis=-1, keepdims=True)
    i2 = first_max(el2)
    e2 = jnp.exp(m2 - m1)
    w1 = gw / (1.0 + e2)
    comb_ref[...] = jnp.where(lane == i1, w1, 0.0) + jnp.where(lane == i2, w1 * e2, 0.0)
    sel_ref[...] = jnp.where((lane == i1) | (lane == i2), 1.0, 0.0)


def _post(x, y, g, bonus, attn, prm, rw, tm):
    n, d = x.shape
    row = lambda w: pl.BlockSpec((tm, w), lambda i: (i, 0))
    attn_slabs = attn.ndim == 3
    at_spec = pl.BlockSpec((attn.shape[0], tm, PAIR), lambda i: (0, i, 0)) if attn_slabs else row(attn.shape[1])
    params = [prm[k] for k in ("ln_x_w", "ln_x_b", "attn_gain", "w_out", "norm_ffn", "wr_hi", "wr_lo", "b_router", "bd")]
    return pl.pallas_call(
        functools.partial(_post_kernel, attn_slabs=attn_slabs, rw=rw),
        grid=(n // tm,),
        in_specs=[row(d), row(rw), row(rw), row(rw), at_spec] + [_full(p.shape) for p in params],
        out_specs=[row(d), row(d), row(ROUTER_LANES), row(ROUTER_LANES)],
        out_shape=[jax.ShapeDtypeStruct((n, d), F32), jax.ShapeDtypeStruct((n, d), BF16)]
                  + [jax.ShapeDtypeStruct((n, ROUTER_LANES), F32)] * 2,
        compiler_params=_cparams(1),
        name="post",
    )(x, y, g, bonus, attn, *params)


MOE_ROW_BLOCK = 128
MOE_VMEM_LIMIT = 62 * 1024 * 1024
MOE_EXPERTS_PER_STEP = 4
MOE_SEG_ALIGN = 16
TOP_K = 2


def _moe_rows(tm):
    rows = TOP_K * tm + N_EXPERTS * (MOE_SEG_ALIGN - 1)
    return -(-rows // LANES) * LANES


def _moe_kernel(hn_ref, comb_ref, sel_ref, tri_ref, wg_ref, wu_ref, wd_ref, o_ref, p_all, xs, ys, cs, meta):
    e = pl.program_id(1)
    tm = hn_ref.shape[0]
    rows = p_all.shape[1]
    tn = lambda a_, b_: lax.dot_general(a_, b_, (((0,), (0,)), ((), ())), preferred_element_type=F32)

    @pl.when(e == 0)
    def _():
        sel = sel_ref[...]
        picked = sel > 0.0
        rank = jnp.dot(tri_ref[...], sel.astype(BF16), preferred_element_type=F32)
        cnt = rank[tm - 1:tm] + sel[tm - 1:tm]
        seg = jnp.ceil(cnt * (1.0 / MOE_SEG_ALIGN))
        li = lax.broadcasted_iota(jnp.int32, (LANES, LANES), 0)
        lj = lax.broadcasted_iota(jnp.int32, (LANES, LANES), 1)
        before = (li < lj).astype(BF16)
        off = jnp.dot(jnp.broadcast_to(seg, (8, LANES)).astype(BF16), before,
                      preferred_element_type=F32)[0:1] * float(MOE_SEG_ALIGN)
        meta[0:1, :] = cnt
        meta[1:2, :] = off
        dest = off + rank
        d_lo = jnp.min(jnp.where(picked, dest, 1e9), axis=-1, keepdims=True)
        d_hi = jnp.max(jnp.where(picked, dest, -1.0), axis=-1, keepdims=True)
        comb = comb_ref[...]
        w_lo = jnp.sum(jnp.where(picked & (dest == d_lo), comb, 0.0), axis=-1, keepdims=True)
        w_hi = jnp.sum(jnp.where(picked & (dest == d_hi), comb, 0.0), axis=-1, keepdims=True)
        d_lo, d_hi = d_lo.astype(jnp.int32), d_hi.astype(jnp.int32)
        hn = hn_ref[...]
        chunk = next(c for c in (256, 128) if rows % c == 0)
        for c0 in range(0, rows, chunk):
            row_id = c0 + lax.broadcasted_iota(jnp.int32, (tm, chunk), 1)
            is_lo = row_id == d_lo
            is_hi = row_id == d_hi
            one = jnp.where(is_lo | is_hi, 1.0, 0.0).astype(BF16)
            p_all[:, c0:c0 + chunk] = one
            xs[c0:c0 + chunk, :] = tn(one, hn).astype(BF16)
            crow = jnp.sum(jnp.where(is_lo, w_lo, 0.0) + jnp.where(is_hi, w_hi, 0.0), axis=0, keepdims=True)
            for k0 in range(0, chunk, LANES):
                diag = jnp.where(li == lj, jnp.broadcast_to(crow[:, k0:k0 + LANES], (LANES, LANES)), 0.0)
                cs[c0 + k0:c0 + k0 + LANES, :] = jnp.broadcast_to(jnp.sum(diag, axis=1, keepdims=True), (LANES, LANES))
        xs[rows:, :] = jnp.zeros((xs.shape[0] - rows, xs.shape[1]), BF16)
        ys[...] = jnp.zeros_like(ys)
        cs[rows:, :] = jnp.zeros((cs.shape[0] - rows, LANES), F32)

    lane = lax.broadcasted_iota(jnp.int32, (1, LANES), 1)
    n_here = wg_ref.shape[0]
    counts, offs, n_blocks = [], [], []
    for j in range(n_here):
        mine = lane == e * n_here + j + EXPERT_LANE0
        counts.append(jnp.sum(jnp.where(mine, meta[0:1, :], 0.0)).astype(jnp.int32))
        offs.append(jnp.sum(jnp.where(mine, meta[1:2, :], 0.0)).astype(jnp.int32))
        n_blocks.append((counts[j] + MOE_ROW_BLOCK - 1) // MOE_ROW_BLOCK)
    row_in_block = lax.broadcasted_iota(jnp.int32, (MOE_ROW_BLOCK, 1), 0)

    def expert_block(j, b):
        bj = jnp.minimum(b, jnp.maximum(n_blocks[j] - 1, 0))
        r = pl.ds(pl.multiple_of(offs[j] + bj * MOE_ROW_BLOCK, MOE_SEG_ALIGN), MOE_ROW_BLOCK)
        x = xs[r, :]
        gate = jnp.dot(x, wg_ref[j], preferred_element_type=F32)
        up = jnp.dot(x, wu_ref[j], preferred_element_type=F32)
        yield
        act = (gate * jax.nn.sigmoid(gate) * up * cs[r, 0:1]).astype(BF16)
        y = jnp.dot(act, wd_ref[j], preferred_element_type=F32).astype(BF16)
        yield
        ys[r, :] = jnp.where(row_in_block < counts[j] - bj * MOE_ROW_BLOCK, y, ys[r, :])

    def blocks(b, carry):
        _lockstep([expert_block(j, b) for j in range(n_here)])
        return carry

    lax.fori_loop(0, functools.reduce(jnp.maximum, n_blocks), blocks, 0)

    @pl.when(e == pl.num_programs(1) - 1)
    def _():
        o_ref[...] = jnp.dot(p_all[...], ys[:rows, :], preferred_element_type=F32)


def _moe(hn, comb, sel, wg, wu, wd, tm):
    n, d = hn.shape
    n_exp, _, de = wg.shape
    rows = _moe_rows(tm)
    tri = jnp.asarray(np.tril(np.ones((tm, tm), np.float32), -1), BF16)
    tile = lambda w: pl.BlockSpec((tm, w), lambda i, e: (i, 0))
    return pl.pallas_call(
        _moe_kernel,
        grid=(n // tm, n_exp // MOE_EXPERTS_PER_STEP),
        in_specs=[tile(d), tile(ROUTER_LANES), tile(ROUTER_LANES),
                  pl.BlockSpec((tm, tm), lambda i, e: (0, 0)),
                  pl.BlockSpec((MOE_EXPERTS_PER_STEP, d, de), lambda i, e: (e, 0, 0)),
                  pl.BlockSpec((MOE_EXPERTS_PER_STEP, d, de), lambda i, e: (e, 0, 0)),
                  pl.BlockSpec((MOE_EXPERTS_PER_STEP, de, d), lambda i, e: (e, 0, 0))],
        out_specs=tile(d),
        out_shape=jax.ShapeDtypeStruct((n, d), F32),
        scratch_shapes=[pltpu.VMEM((tm, rows), BF16),
                        pltpu.VMEM((rows + MOE_ROW_BLOCK, d), BF16), pltpu.VMEM((rows + MOE_ROW_BLOCK, d), BF16),
                        pltpu.VMEM((rows + MOE_ROW_BLOCK, LANES), F32), pltpu.VMEM((8, LANES), F32)],
        compiler_params=_cparams(2, MOE_VMEM_LIMIT),
        name="moe",
    )(hn, comb, sel, tri, wg, wu, wd)


def _tail_kernel(x1_ref, moe_ref, p_ref, nple_ref, wgate_ref, wple_ref, nfin_ref, y_ref):
    x2 = x1_ref[...] + moe_ref[...]
    gate = jax.nn.sigmoid(jnp.dot(_rms(x2, nple_ref[...]).astype(BF16), wgate_ref[...], preferred_element_type=F32))
    x3 = x2 + jnp.dot(p_ref[...].astype(BF16), wple_ref[...], preferred_element_type=F32) * gate
    y_ref[...] = _rms(x3, nfin_ref[...])


def _tail(x1, moe, p, prm, tm):
    n, d = x1.shape
    row = lambda w: pl.BlockSpec((tm, w), lambda i: (i, 0))
    params = [prm[k] for k in ("norm_ple", "w_ple_gate", "w_ple", "norm_final")]
    return pl.pallas_call(
        _tail_kernel,
        grid=(n // tm,),
        in_specs=[row(d), row(d), row(p.shape[1])] + [_full(a.shape) for a in params],
        out_specs=row(d),
        out_shape=jax.ShapeDtypeStruct((n, d), F32),
        compiler_params=_cparams(1),
        name="tail",
    )(x1, moe, p, *params)


def _layer_params(i, norm_mix, w_in, mu, w0, w2, a0, a2, g2, k_k, k_a, r_k, ln_x_w, ln_x_b, attn_gain, w_out,
                  norm_ffn, w_group, b_group, w_expert_router, b_expert_router, w_gate, w_up, w_down,
                  norm_ple, w_ple, w_ple_gate, norm_final):
    rw = w0.shape[1]
    d = w_in.shape[1]
    vec = lambda a: a[i].reshape(1, -1)
    lora_rows = DECAY_LORA + AAA_LORA + GATE_LORA

    def lora(w, lo):
        return jnp.zeros((lora_rows, rw), F32).at[lo:lo + w.shape[0]].set(w).astype(BF16)

    head = np.arange(rw) // HEAD_DIM
    w_router = jnp.zeros((d, ROUTER_LANES), F32)
    w_router = w_router.at[:, :N_GROUPS].set(w_group[i]).at[:, EXPERT_LANE0:EXPERT_LANE0 + N_EXPERTS].set(w_expert_router[i])
    wr_hi, wr_lo = _split2(w_router)
    b_router = jnp.zeros((1, ROUTER_LANES), F32)
    b_router = b_router.at[0, :N_GROUPS].set(b_group[i]).at[0, EXPERT_LANE0:EXPERT_LANE0 + N_EXPERTS].set(b_expert_router[i])
    return dict(
        norm_mix=vec(norm_mix), w_in=w_in[i].astype(BF16), mu=vec(mu), w0=vec(w0), a0=vec(a0),
        w_kv_t=w_in[i][:, w_in.shape[2] - 2 * ((w_in.shape[2] - mu.shape[1]) // 3):].T.astype(BF16),
        w2p=lora(w2[i], 0), a2p=lora(a2[i], DECAY_LORA), g2p=lora(g2[i], DECAY_LORA + AAA_LORA),
        k_k=vec(k_k), k_a=vec(k_a), r_k=vec(r_k), ln_x_w=vec(ln_x_w), ln_x_b=vec(ln_x_b),
        bd=jnp.asarray(head[:, None] == head[None, :], BF16),
        attn_gain=vec(attn_gain), w_out=w_out[i].astype(BF16), norm_ffn=vec(norm_ffn),
        wr_hi=wr_hi, wr_lo=wr_lo, b_router=b_router,
        w_gate=w_gate[i].astype(BF16), w_up=w_up[i].astype(BF16), w_down=w_down[i].astype(BF16),
        norm_ple=vec(norm_ple), w_ple=w_ple[i].astype(BF16), w_ple_gate=w_ple_gate[i].astype(BF16),
        norm_final=norm_final.reshape(1, -1),
    )


def _tile(n, want):
    t = min(n, want)
    assert n % t == 0, (n, t)
    return t


def _ffn_and_tail(x, y, g, bonus, attn, p, prm, rw):
    n = x.shape[0]
    x1, hn, comb, sel = _post(x, y, g, bonus, attn, prm, rw, _tile(n, 512))
    moe = _moe(hn, comb, sel, prm["w_gate"], prm["w_up"], prm["w_down"], _tile(n, 1024))
    return _tail(x1, moe, p, prm, _tile(n, 512))


def kernel(x_prompt, x_sample, state_wkv, state_shift, cache_k_win, cache_v_win, p_prompt, p_sample, norm_mix, w_in, mu, w0, w2, a0, a2, g2, k_k, k_a, r_k, ln_x_w, ln_x_b, attn_gain, w_out, norm_ffn, w_group, b_group, w_expert_router, b_expert_router, w_gate, w_up, w_down, norm_ple, w_ple, w_ple_gate, norm_final):
    depth = norm_mix.shape[0]
    assert depth == 1, "single-layer step"
    bp, s_len, d = x_prompt.shape
    bs, t_len, _ = x_sample.shape
    rw = w0.shape[1]
    n_rw_heads = rw // HEAD_DIM
    rw_cols = mu.shape[1]
    at_width = (w_in.shape[2] - rw_cols) // 3
    n_at_heads = at_width // HEAD_DIM
    win = cache_k_win.shape[2]
    assert s_len % (max(DILATIONS) * ATT_BLOCK) == 0 and s_len >= win and t_len == 8 and bs % 8 == 0
    assert all(wd // dl == ATT_BLOCK for wd, dl in zip(WINDOWS, DILATIONS)) and win >= max(WINDOWS)

    prm = _layer_params(0, norm_mix, w_in, mu, w0, w2, a0, a2, g2, k_k, k_a, r_k, ln_x_w, ln_x_b, attn_gain, w_out,
                        norm_ffn, w_group, b_group, w_expert_router, b_expert_router, w_gate, w_up, w_down,
                        norm_ple, w_ple, w_ple_gate, norm_final)

    n_p = bp * s_len
    xp = x_prompt.reshape(n_p, d)
    tm = _tile(s_len, 512)
    q, k, v, r_, lw, km, vv, kk, kka, g, bonus, shift_p = _proj_prep(
        xp, jnp.zeros((8, rw_cols), F32), prm, rw_cols, at_width, rw, tm, True, False, s_len // tm)
    ct = _tile(s_len, 512)
    y, wkv_p = _wkv_prompt(bp, n_rw_heads, (r_, lw, km, vv, kk, kka), ct)
    attn = _attn_prompt(q, k, v, bp, n_at_heads)
    y_prompt = _ffn_and_tail(xp, y, g, bonus, attn, p_prompt[0].reshape(n_p, -1), prm, rw).reshape(bp, s_len, d)
    shift_p = shift_p.reshape(bp, rw_cols)
    keep = min(win, s_len)
    kv_t = _win_proj(x_prompt, prm["norm_mix"], prm["w_kv_t"], keep, _tile(keep, 512))
    window = lambda z: jnp.transpose(z.reshape(bp, n_at_heads, HEAD_DIM, keep), (0, 3, 1, 2))
    kwin_p, vwin_p = window(kv_t[:, :at_width]), window(kv_t[:, at_width:])

    n_s = bs * t_len
    xs = jnp.swapaxes(x_sample, 0, 1).reshape(n_s, d)
    ps = jnp.swapaxes(p_sample[0], 0, 1).reshape(n_s, -1)
    q_s, k_s, v_s, r_, lw, km, vv, kk, kka, g, bonus, shift_s = _proj_prep(
        xs, state_shift[0], prm, rw_cols, at_width, rw, bs, False, True, t_len)
    y, wkv_s = _wkv_sample(t_len, bs, n_rw_heads, (r_, lw, km, vv, kk, kka), state_wkv[0], 8)
    to_t = lambda c: jnp.transpose(c[0], (0, 2, 3, 1)).reshape(bs, at_width, win)
    from_t = lambda c: jnp.transpose(c.reshape(bs, n_at_heads, HEAD_DIM, win), (0, 3, 1, 2))[None]
    at_s, kwin_s, vwin_s = _attn_sample(q_s, k_s, v_s, to_t(cache_k_win), to_t(cache_v_win), t_len, bs, n_at_heads)
    y_s = _ffn_and_tail(xs, y, g, bonus, at_s, ps, prm, rw)
    y_sample = jnp.swapaxes(y_s.reshape(t_len, bs, d), 0, 1)

    return (y_prompt, y_sample, wkv_p[None], shift_p[None], kwin_p[None], vwin_p[None],
            wkv_s[None], shift_s[None], from_t(kwin_s), from_t(vwin_s))
```
